```python
import functools
import jax, jax.numpy as jnp
from jax import lax
import numpy as np

D_MODEL = 1024
BATCH = 2
SEQ = 8192
DEPTH = 2
DEC_BATCH = 32
DEC_SEQ = 8
PAST_LEN = 16384
PAGE_SIZE = 128

F32 = jnp.float32
BRANCH_W = D_MODEL // 2
N_BRANCH = 3
CONV_CH = BRANCH_W
CONV_WIDTH = 31
RWKV_HD = 64
RWKV_HEADS = BRANCH_W // RWKV_HD
RWKV_W = RWKV_HEADS * RWKV_HD
LORA_W = 64
LORA_A = 64
LORA_G = 128
GN_EPS = 64e-5
ATT_HD = 64
ATT_HEADS = BRANCH_W // ATT_HD
ATT_W = ATT_HEADS * ATT_HD
MOBA_BLOCK = 256
MOBA_TOPK = 3
Q_BLOCK = 128
N_GROUPS = 4
EXPERTS_PER_GROUP = 4
N_EXPERTS = N_GROUPS * EXPERTS_PER_GROUP
TOP_K_INNER = 2
D_EXPERT = D_MODEL // 2
EPS = 1e-6
A_COLS = 2 * CONV_CH
B_COLS = 3 * RWKV_W + LORA_W + LORA_A + LORA_G
C_COLS = 3 * ATT_W
G_COLS = N_BRANCH * D_MODEL
OFF_B = A_COLS
OFF_C = OFF_B + B_COLS
OFF_G = OFF_C + C_COLS
N_COLS = OFF_G + G_COLS

kernel_name = 'hybrid_conv_rwkv7_moba_hmoe_step'


def rmsnorm(x, g):
    xf = x.astype(F32)
    y = xf * lax.rsqrt(jnp.mean(xf * xf, axis=-1, keepdims=True) + EPS)
    return (y * g.astype(F32)).astype(x.dtype)


def layernorm(x, g, b):
    xf = x.astype(F32)
    mu = jnp.mean(xf, axis=-1, keepdims=True)
    var = jnp.mean(jnp.square(xf - mu), axis=-1, keepdims=True)
    y = (xf - mu) * lax.rsqrt(var + 1e-5) * g.astype(F32) + b.astype(F32)
    return y.astype(x.dtype)


def alibi_slopes():
    return jnp.exp2(-8.0 * (jnp.arange(ATT_HEADS, dtype=F32) + 1.0) / ATT_HEADS)


def conv_branch(u, buf, prm):
    h = u[..., :CONV_CH] * jax.nn.sigmoid(u[..., CONV_CH:])
    hc = jnp.concatenate([buf.astype(h.dtype), h], axis=1)
    y = lax.conv_general_dilated(hc, prm['conv_w'][:, None, :].astype(h.dtype), (1,), 'VALID',
                                 dimension_numbers=('NWC', 'WIO', 'NWC'), feature_group_count=CONV_CH)
    y = layernorm(y + prm['conv_b'].astype(h.dtype), prm['conv_ln_g'], prm['conv_ln_b'])
    return jax.nn.silu(y), hc[:, -(CONV_WIDTH - 1):]


def wkv_step(S, inp):
    r_t, w_t, k_t, v_t, kk_t, a_t = inp
    sk = jnp.einsum('bhij,bhj->bhi', S, kk_t)
    S = S * w_t[:, :, None, :] - sk[..., None] * (kk_t * a_t)[:, :, None, :] + v_t[..., None] * k_t[:, :, None, :]
    return S, jnp.einsum('bhij,bhj->bhi', S, r_t)


def rwkv_branch(p, prev, wkv0, prm):
    B, T, _ = p.shape
    p_prev = jnp.concatenate([prev[:, None, :].astype(p.dtype), p[:, :-1]], axis=1)
    xm = p + (p_prev - p) * prm['rwkv_mu'].astype(p.dtype)
    W = RWKV_W
    r = xm[..., :W]
    k = xm[..., W:2 * W]
    v = xm[..., 2 * W:3 * W]
    wd = xm[..., 3 * W:3 * W + LORA_W]
    ad = xm[..., 3 * W + LORA_W:3 * W + LORA_W + LORA_A]
    gd = xm[..., 3 * W + LORA_W + LORA_A:]
    w_raw = prm['rwkv_w0'].astype(F32) + (jnp.tanh(wd) @ prm['rwkv_w_up']).astype(F32)
    decay = jnp.exp(-jnp.exp(-jax.nn.softplus(-w_raw) - 0.5))
    a = jax.nn.sigmoid(prm['rwkv_a0'].astype(F32) + (ad @ prm['rwkv_a_up']).astype(F32))
    g = (jax.nn.sigmoid(gd) @ prm['rwkv_g_up']).astype(F32)
    hs = (B, T, RWKV_HEADS, RWKV_HD)
    hn = (RWKV_HEADS, RWKV_HD)
    r = r.astype(F32).reshape(hs)
    k = k.astype(F32).reshape(hs)
    v = v.astype(F32).reshape(hs)
    a = a.reshape(hs)
    decay = decay.reshape(hs)
    kk = k * prm['rwkv_k_k'].astype(F32).reshape(hn)
    kk = kk / jnp.maximum(jnp.linalg.norm(kk, axis=-1, keepdims=True), 1e-12)
    k = k * (1.0 + (a - 1.0) * prm['rwkv_k_a'].astype(F32).reshape(hn))
    seq = tuple(jnp.moveaxis(t, 1, 0) for t in (r, decay, k, v, kk, a))
    S, o = lax.scan(wkv_step, wkv0.astype(F32), seq)
    o = jnp.moveaxis(o, 0, 1)
    mu = jnp.mean(o, axis=-1, keepdims=True)
    var = jnp.mean(jnp.square(o - mu), axis=-1, keepdims=True)
    o = (o - mu) * lax.rsqrt(var + GN_EPS) * prm['rwkv_gn_g'].astype(F32).reshape(hn) + prm['rwkv_gn_b'].astype(F32).reshape(hn)
    o = o + jnp.sum(r * k * prm['rwkv_r_k'].astype(F32), axis=-1, keepdims=True) * v
    o = o.reshape(B, T, W) * g
    return o.astype(p.dtype), p[:, -1], S.astype(p.dtype)


def moba_prompt(q, k, v, slopes):
    B, S, H, D = q.shape
    nb = -(-S // MOBA_BLOCK)
    nb_g = max(nb, MOBA_TOPK)
    s_pad = nb_g * MOBA_BLOCK
    pad = ((0, 0), (0, 0), (0, s_pad - S), (0, 0))
    qt = jnp.swapaxes(q, 1, 2).astype(F32)
    kt = jnp.pad(jnp.swapaxes(k, 1, 2).astype(F32), pad)
    vt = jnp.pad(jnp.swapaxes(v, 1, 2).astype(F32), pad)
    kb = kt.reshape(B, H, nb_g, MOBA_BLOCK, D)
    vb = vt.reshape(B, H, nb_g, MOBA_BLOCK, D)
    kmean = jnp.mean(kb, axis=3)
    bi = jnp.arange(B)[:, None, None, None]
    hi = jnp.arange(H)[None, :, None, None]
    blk = jnp.arange(nb_g)
    rank = jnp.arange(MOBA_TOPK)
    rows = jnp.arange(MOBA_BLOCK)
    sl = slopes[None, :, None, None]
    scale = ATT_HD ** -0.5
    n_sel = MOBA_TOPK * MOBA_BLOCK

    def one_block(i):
        start = i * Q_BLOCK
        q_i = lax.dynamic_slice_in_dim(qt, start, Q_BLOCK, axis=2)
        t = (start + jnp.arange(Q_BLOCK)).astype(F32)
        j_own = start // MOBA_BLOCK
        gate = jnp.einsum('bhqd,bhnd->bhqn', q_i, kmean)
        gate = jnp.where(blk < j_own, gate, -jnp.inf)
        _, idx = lax.top_k(gate, MOBA_TOPK)
        ks = kb[bi, hi, idx]
        vs = vb[bi, hi, idx]
        pos = (idx[..., None] * MOBA_BLOCK + rows).astype(F32)
        s_sel = jnp.einsum('bhqd,bhqkld->bhqkl', q_i, ks) * scale - sl[..., None] * (t[:, None, None] - pos)
        s_sel = jnp.where((rank < j_own)[:, None], s_sel, -jnp.inf)
        k_own = lax.dynamic_slice_in_dim(kt, j_own * MOBA_BLOCK, MOBA_BLOCK, axis=2)
        v_own = lax.dynamic_slice_in_dim(vt, j_own * MOBA_BLOCK, MOBA_BLOCK, axis=2)
        pos_own = (j_own * MOBA_BLOCK + rows).astype(F32)
        s_own = jnp.einsum('bhqd,bhld->bhql', q_i, k_own) * scale - sl * (t[:, None] - pos_own)
        s_own = jnp.where(pos_own <= t[:, None], s_own, -jnp.inf)
        logits = jnp.concatenate([s_sel.reshape(B, H, Q_BLOCK, n_sel), s_own], axis=-1)
        prob = jax.nn.softmax(logits, axis=-1)
        p_sel = prob[..., :n_sel].reshape(B, H, Q_BLOCK, MOBA_TOPK, MOBA_BLOCK)
        return (jnp.einsum('bhqkl,bhqkld->bhqd', p_sel, vs)
                + jnp.einsum('bhql,bhld->bhqd', prob[..., n_sel:], v_own))

    o = lax.map(one_block, jnp.arange(S // Q_BLOCK))
    return jnp.transpose(o, (1, 0, 3, 2, 4)).reshape(B, S, H * D)


def moba_sample(q, k, v, cache_k, cache_v, layer, page_table, slopes):
    DB, T, H, D = q.shape
    n_pages = page_table.shape[1]
    past = n_pages * PAGE_SIZE
    ppb = MOBA_BLOCK // PAGE_SIZE
    nb_past = past // MOBA_BLOCK
    own_pages = n_pages - nb_past * ppb
    t = (past + jnp.arange(T)).astype(F32)
    qf = q.astype(F32)
    scale = ATT_HD ** -0.5
    logit_parts, value_parts = [], []
    if nb_past > 0:
        k_rows = cache_k[layer, page_table[:, :nb_past * ppb]].astype(F32)
        kmean = jnp.mean(k_rows.reshape(DB, nb_past, MOBA_BLOCK, H, D), axis=2)
        gate = jnp.einsum('bthd,bnhd->bthn', qf, kmean)
        n_sel = min(MOBA_TOPK, nb_past)
        _, idx = lax.top_k(gate, n_sel)
        lp = idx[..., None] * ppb + jnp.arange(ppb)
        pids = page_table[jnp.arange(DB)[:, None, None, None, None], lp]
        hi = jnp.arange(H)[None, None, :, None, None]
        ks = cache_k[layer, pids, :, hi].astype(F32)
        vs = cache_v[layer, pids, :, hi].astype(F32)
        pos = (lp[..., None] * PAGE_SIZE + jnp.arange(PAGE_SIZE)).astype(F32)
        s = (jnp.einsum('bthd,bthnprd->bthnpr', qf, ks) * scale
             - slopes[None, None, :, None, None, None] * (t[None, :, None, None, None, None] - pos))
        logit_parts.append(s.reshape(DB, T, H, n_sel * MOBA_BLOCK))
        value_parts.append(('bthl,bthld->bthd', vs.reshape(DB, T, H, n_sel * MOBA_BLOCK, D)))
    if own_pages > 0:
        n_rows = own_pages * PAGE_SIZE
        k_op = cache_k[layer, page_table[:, nb_past * ppb:]].astype(F32).reshape(DB, n_rows, H, D)
        v_op = cache_v[layer, page_table[:, nb_past * ppb:]].astype(F32).reshape(DB, n_rows, H, D)
        pos_op = (past - n_rows + jnp.arange(n_rows)).astype(F32)
        s = (jnp.einsum('bthd,brhd->bthr', qf, k_op) * scale
             - slopes[None, None, :, None] * (t[:, None] - pos_op)[None, :, None, :])
        logit_parts.append(s)
        value_parts.append(('bthr,brhd->bthd', v_op))
    s_new = (jnp.einsum('bthd,buhd->bthu', qf, k.astype(F32)) * scale
             - slopes[None, None, :, None] * (t[:, None] - t[None, :])[None, :, None, :])
    causal = (jnp.arange(T)[:, None] >= jnp.arange(T)[None, :])[None, :, None, :]
    logit_parts.append(jnp.where(causal, s_new, -jnp.inf))
    value_parts.append(('bthu,buhd->bthd', v.astype(F32)))
    prob = jax.nn.softmax(jnp.concatenate(logit_parts, axis=-1), axis=-1)
    splits = [int(c) for c in np.cumsum([lpart.shape[-1] for lpart in logit_parts])[:-1]]
    out = jnp.zeros((DB, T, H, D), F32)
    for pr, (spec, val) in zip(jnp.split(prob, splits, axis=-1), value_parts):
        out = out + jnp.einsum(spec, pr, val)
    return out.reshape(DB, T, H * D)


def hier_moe(h, prm):
    n = h.shape[0]
    g_logit = (h @ prm['moe_w_group']).astype(F32) + prm['moe_b_group'].astype(F32)
    g_idx = jnp.argmax(g_logit, axis=-1)
    g_w = jnp.max(jax.nn.softmax(g_logit, axis=-1), axis=-1)
    e_logit = ((h @ prm['moe_w_expert']).astype(F32) + prm['moe_b_expert'].astype(F32)).reshape(n, N_GROUPS, EXPERTS_PER_GROUP)
    e_logit = jnp.take_along_axis(e_logit, g_idx[:, None, None], axis=1)[:, 0]
    top_v, top_i = lax.top_k(e_logit, TOP_K_INNER)
    w_sel = jax.nn.softmax(top_v, axis=-1) * g_w[:, None]
    eid = g_idx[:, None] * EXPERTS_PER_GROUP + top_i
    combine = jnp.sum(jax.nn.one_hot(eid, N_EXPERTS, dtype=F32) * w_sel[..., None], axis=1)
    hg = jnp.einsum('nd,edf->nef', h, prm['moe_w_gate'])
    hu = jnp.einsum('nd,edf->nef', h, prm['moe_w_up'])
    act = jax.nn.silu(hg) * hu * combine[..., None].astype(h.dtype)
    return jnp.einsum('nef,efd->nd', act, prm['moe_w_down'])


def trunk_layer(x, prm, conv_buf, shift_prev, wkv_prev, attend):
    B, T, _ = x.shape
    h = rmsnorm(x, prm['norm1_g'])
    proj = jnp.einsum('btd,dc->btc', h, prm['w_in'])
    gates = jax.nn.sigmoid(proj[..., OFF_G:].astype(F32)).reshape(B, T, N_BRANCH, D_MODEL)
    o_a, conv_new = conv_branch(proj[..., :OFF_B], conv_buf, prm)
    o_b, shift_new, wkv_new = rwkv_branch(proj[..., OFF_B:OFF_C], shift_prev, wkv_prev, prm)
    q, k, v = jnp.split(proj[..., OFF_C:OFF_G], 3, axis=-1)
    q = rmsnorm(q.reshape(B, T, ATT_HEADS, ATT_HD), prm['q_norm_g'])
    k = rmsnorm(k.reshape(B, T, ATT_HEADS, ATT_HD), prm['k_norm_g'])
    v = v.reshape(B, T, ATT_HEADS, ATT_HD)
    o_c = attend(q, k, v).astype(x.dtype)
    branches = jnp.stack([o_a.astype(x.dtype), o_b, o_c], axis=2)
    br = jnp.einsum('btnc,ncd->btnd', branches, prm['w_branch'])
    merged = jnp.sum(gates * br.astype(F32), axis=2).astype(x.dtype)
    x = x + jnp.einsum('btd,de->bte', merged, prm['w_out'])
    h2 = rmsnorm(x, prm['norm2_g']).reshape(B * T, D_MODEL)
    x = x + hier_moe(h2, prm).reshape(B, T, D_MODEL)
    return x, k, v, conv_new, shift_new, wkv_new


def setup_inputs(seed: int = 0) -> dict:
    key = jax.random.key(seed)
    keys = iter(jax.random.split(key, 48))

    def nrm(shape, scale):
        return jax.random.normal(next(keys), shape, F32) * scale

    n_pages = PAST_LEN // PAGE_SIZE
    n_pool = (DEC_BATCH * n_pages * 5) // 4
    perm = jax.random.permutation(next(keys), n_pool)
    page_table = perm[:DEC_BATCH * n_pages].reshape(DEC_BATCH, n_pages).astype(jnp.int32)
    return {
        'x_prompt': nrm((BATCH, SEQ, D_MODEL), 1.0),
        'x_sample': nrm((DEC_BATCH, DEC_SEQ, D_MODEL), 1.0),
        'cache_k': nrm((DEPTH, n_pool, PAGE_SIZE, ATT_HEADS, ATT_HD), 1.0),
        'cache_v': nrm((DEPTH, n_pool, PAGE_SIZE, ATT_HEADS, ATT_HD), 1.0),
        'page_table': page_table,
        'state_conv': nrm((DEPTH, DEC_BATCH, CONV_WIDTH - 1, CONV_CH), 0.5),
        'state_shift': nrm((DEPTH, DEC_BATCH, B_COLS), 1.0),
        'state_wkv': nrm((DEPTH, DEC_BATCH, RWKV_HEADS, RWKV_HD, RWKV_HD), 0.3),
        'norm1_g': 1.0 + nrm((DEPTH, D_MODEL), 0.02),
        'w_in': nrm((DEPTH, D_MODEL, N_COLS), D_MODEL ** -0.5),
        'conv_w': nrm((DEPTH, CONV_WIDTH, CONV_CH), CONV_WIDTH ** -0.5),
        'conv_b': nrm((DEPTH, CONV_CH), 0.02),
        'conv_ln_g': 1.0 + nrm((DEPTH, CONV_CH), 0.02),
        'conv_ln_b': nrm((DEPTH, CONV_CH), 0.02),
        'rwkv_mu': jax.random.uniform(next(keys), (DEPTH, B_COLS), F32, 0.0, 1.0),
        'rwkv_w0': jax.random.uniform(next(keys), (DEPTH, RWKV_W), F32, -6.0, 1.0),
        'rwkv_w_up': nrm((DEPTH, LORA_W, RWKV_W), 0.1),
        'rwkv_a0': nrm((DEPTH, RWKV_W), 0.1),
        'rwkv_a_up': nrm((DEPTH, LORA_A, RWKV_W), 0.5 * LORA_A ** -0.5),
        'rwkv_g_up': nrm((DEPTH, LORA_G, RWKV_W), LORA_G ** -0.5),
        'rwkv_k_k': 0.85 + nrm((DEPTH, RWKV_W), 0.02),
        'rwkv_k_a': 1.0 + nrm((DEPTH, RWKV_W), 0.02),
        'rwkv_r_k': nrm((DEPTH, RWKV_HEADS, RWKV_HD), 0.1),
        'rwkv_gn_g': 1.0 + nrm((DEPTH, RWKV_W), 0.02),
        'rwkv_gn_b': nrm((DEPTH, RWKV_W), 0.02),
        'q_norm_g': 1.0 + nrm((DEPTH, ATT_HD), 0.02),
        'k_norm_g': 1.0 + nrm((DEPTH, ATT_HD), 0.02),
        'w_branch': nrm((DEPTH, N_BRANCH, BRANCH_W, D_MODEL), BRANCH_W ** -0.5),
        'w_out': nrm((DEPTH, D_MODEL, D_MODEL), D_MODEL ** -0.5),
        'norm2_g': 1.0 + nrm((DEPTH, D_MODEL), 0.02),
        'moe_w_group': nrm((DEPTH, D_MODEL, N_GROUPS), D_MODEL ** -0.5),
        'moe_b_group': nrm((DEPTH, N_GROUPS), 0.01),
        'moe_w_expert': nrm((DEPTH, D_MODEL, N_EXPERTS), D_MODEL ** -0.5),
        'moe_b_expert': nrm((DEPTH, N_EXPERTS), 0.01),
        'moe_w_gate': nrm((DEPTH, N_EXPERTS, D_MODEL, D_EXPERT), D_MODEL ** -0.5),
        'moe_w_up': nrm((DEPTH, N_EXPERTS, D_MODEL, D_EXPERT), D_MODEL ** -0.5),
        'moe_w_down': nrm((DEPTH, N_EXPERTS, D_EXPERT, D_MODEL), D_EXPERT ** -0.5),
    }


def reference(x_prompt, x_sample, cache_k, cache_v, page_table, state_conv, state_shift, state_wkv,
              norm1_g, w_in, conv_w, conv_b, conv_ln_g, conv_ln_b,
              rwkv_mu, rwkv_w0, rwkv_w_up, rwkv_a0, rwkv_a_up, rwkv_g_up, rwkv_k_k, rwkv_k_a, rwkv_r_k,
              rwkv_gn_g, rwkv_gn_b, q_norm_g, k_norm_g, w_branch, w_out, norm2_g,
              moe_w_group, moe_b_group, moe_w_expert, moe_b_expert, moe_w_gate, moe_w_up, moe_w_down):
    slopes = alibi_slopes()
    B, S, _ = x_prompt.shape
    xp, xs = x_prompt, x_sample
    kp_l, vp_l, ks_l, vs_l, cp_l, cs_l, sp_l, ss_l, wp_l, ws_l = ([] for _ in range(10))
    for l in range(DEPTH):
        prm = dict(norm1_g=norm1_g[l], w_in=w_in[l], conv_w=conv_w[l], conv_b=conv_b[l],
                   conv_ln_g=conv_ln_g[l], conv_ln_b=conv_ln_b[l], rwkv_mu=rwkv_mu[l], rwkv_w0=rwkv_w0[l],
                   rwkv_w_up=rwkv_w_up[l], rwkv_a0=rwkv_a0[l], rwkv_a_up=rwkv_a_up[l], rwkv_g_up=rwkv_g_up[l],
                   rwkv_k_k=rwkv_k_k[l], rwkv_k_a=rwkv_k_a[l], rwkv_r_k=rwkv_r_k[l], rwkv_gn_g=rwkv_gn_g[l],
                   rwkv_gn_b=rwkv_gn_b[l], q_norm_g=q_norm_g[l], k_norm_g=k_norm_g[l], w_branch=w_branch[l],
                   w_out=w_out[l], norm2_g=norm2_g[l], moe_w_group=moe_w_group[l], moe_b_group=moe_b_group[l],
                   moe_w_expert=moe_w_expert[l], moe_b_expert=moe_b_expert[l], moe_w_gate=moe_w_gate[l],
                   moe_w_up=moe_w_up[l], moe_w_down=moe_w_down[l])
        conv0 = jnp.zeros((B, CONV_WIDTH - 1, CONV_CH), xp.dtype)
        shift0 = jnp.zeros((B, B_COLS), xp.dtype)
        wkv0 = jnp.zeros((B, RWKV_HEADS, RWKV_HD, RWKV_HD), F32)
        attend_p = functools.partial(moba_prompt, slopes=slopes)
        xp, kp, vp, cp, sp, wp = trunk_layer(xp, prm, conv0, shift0, wkv0, attend_p)
        attend_s = functools.partial(moba_sample, cache_k=cache_k, cache_v=cache_v, layer=l,
                                     page_table=page_table, slopes=slopes)
        xs, ks, vs, cs, ss, ws = trunk_layer(xs, prm, state_conv[l], state_shift[l], state_wkv[l], attend_s)
        kp_l.append(kp.reshape(B, S // PAGE_SIZE, PAGE_SIZE, ATT_HEADS, ATT_HD))
        vp_l.append(vp.reshape(B, S // PAGE_SIZE, PAGE_SIZE, ATT_HEADS, ATT_HD))
        ks_l.append(ks)
        vs_l.append(vs)
        cp_l.append(cp)
        cs_l.append(cs)
        sp_l.append(sp)
        ss_l.append(ss)
        wp_l.append(wp)
        ws_l.append(ws)
    new_k_prompt = jnp.stack(kp_l)
    new_v_prompt = jnp.stack(vp_l)
    new_k_sample = jnp.stack(ks_l)
    new_v_sample = jnp.stack(vs_l)
    new_conv_prompt = jnp.stack(cp_l)
    new_conv_sample = jnp.stack(cs_l)
    new_shift_prompt = jnp.stack(sp_l)
    new_shift_sample = jnp.stack(ss_l)
    new_wkv_prompt = jnp.stack(wp_l)
    new_wkv_sample = jnp.stack(ws_l)
    return (xp, xs, new_k_prompt, new_v_prompt, new_k_sample, new_v_sample,
            new_conv_prompt, new_conv_sample, new_shift_prompt, new_shift_sample,
            new_wkv_prompt, new_wkv_sample)
```

```python
import functools
import math

import jax
import jax.numpy as jnp
from jax import lax
from jax.experimental import pallas as pl
from jax.experimental.pallas import tpu as pltpu

F32 = jnp.float32
BF16 = jnp.bfloat16
HIGHEST = lax.Precision.HIGHEST

HEAD = 64
N_HEADS = 8
BRANCH_W = N_HEADS * HEAD
CONV_WIDTH = 31
LORA = 64
LORA_G = 128
B_COLS = 3 * BRANCH_W + 2 * LORA + LORA_G
MOBA_BLOCK = 256
MOBA_TOPK = 3
Q_BLOCK = 128
PAGE = 128
N_GROUPS = 4
PER_GROUP = 4
N_EXPERTS = N_GROUPS * PER_GROUP
EPS = 1e-6
GN_EPS = 64e-5
NEG = -1e30
MXU_HALF = 256
VMEM_LIMIT = 56 * 1024 * 1024


def _sigmoid(x):
    return 1.0 / (1.0 + jnp.exp(-x))


def _split_dot(x, e, passes):
    acc = None
    rem = x
    for i in range(passes):
        hi = rem.astype(BF16)
        d = jnp.dot(hi, e, preferred_element_type=F32)
        acc = d if acc is None else acc + d
        if i + 1 < passes:
            rem = rem - hi.astype(F32)
    return acc


def _seg_sum(x, e256, passes):
    parts = [_split_dot(x[:, i:i + MXU_HALF], e256, passes) for i in range(0, x.shape[1], MXU_HALF)]
    return jnp.concatenate(parts, axis=1)


def _params(*sem):
    return pltpu.CompilerParams(dimension_semantics=sem, vmem_limit_bytes=VMEM_LIMIT)


def _full(shape):
    n = len(shape)
    return pl.BlockSpec(shape, lambda *_: (0,) * n)


def _inproj_kernel(x_ref, g_ref, w_ref, qg_ref, kg_ref, e_ref, hglu_ref, p_ref, q_ref, k_ref, v_ref):
    x = x_ref[...]
    h = (x * lax.rsqrt(jnp.mean(x * x, axis=-1, keepdims=True) + EPS) * g_ref[...]).astype(BF16)
    c0, c1, c2 = 2 * BRANCH_W, 2 * BRANCH_W + B_COLS, 2 * BRANCH_W + B_COLS + 3 * BRANCH_W
    u = jnp.dot(h, w_ref[:, :c0], preferred_element_type=F32)
    hglu_ref[...] = u[:, :BRANCH_W] * _sigmoid(u[:, BRANCH_W:])
    p_ref[...] = jnp.dot(h, w_ref[:, c0:c1], preferred_element_type=F32)
    qkv = jnp.dot(h, w_ref[:, c1:c2], preferred_element_type=F32)
    e = e_ref[...]
    q = qkv[:, :BRANCH_W]
    k = qkv[:, BRANCH_W:2 * BRANCH_W]
    q_ref[...] = q * lax.rsqrt(_seg_sum(q * q, e, 2) * (1.0 / HEAD) + EPS) * qg_ref[...]
    k_ref[...] = k * lax.rsqrt(_seg_sum(k * k, e, 2) * (1.0 / HEAD) + EPS) * kg_ref[...]
    v_ref[...] = qkv[:, 2 * BRANCH_W:]


def _inproj(x2, g1, w_abc, qg, kg, e256, tm):
    n, d = x2.shape
    row = lambda w: pl.BlockSpec((tm, w), lambda i: (i, 0))
    outs = [BRANCH_W, B_COLS, BRANCH_W, BRANCH_W, BRANCH_W]
    return pl.pallas_call(
        _inproj_kernel,
        grid=(n // tm,),
        in_specs=[row(d), _full(g1.shape), _full(w_abc.shape), _full(qg.shape), _full(kg.shape), _full(e256.shape)],
        out_specs=[row(w) for w in outs],
        out_shape=[jax.ShapeDtypeStruct((n, w), F32) for w in outs],
        compiler_params=_params("parallel"),
        name="inproj",
    )(x2, g1, w_abc, qg, kg, e256)


HALO = 32


def _conv_kernel(h_ref, halo_ref, buf_ref, cw_ref, cb_ref, lg_ref, lb_ref, o_ref, ext):
    i = pl.program_id(1)
    tt = h_ref.shape[1]

    @pl.when(i == 0)
    def _():
        ext[0:HALO, :] = buf_ref[0]

    @pl.when(i > 0)
    def _():
        ext[0:HALO, :] = halo_ref[0]

    ext[HALO:, :] = h_ref[0]
    first = HALO - (CONV_WIDTH - 1)
    acc = ext[first:first + tt, :] * cw_ref[0:1, :]
    for w in range(1, CONV_WIDTH):
        acc = acc + ext[first + w:first + w + tt, :] * cw_ref[w:w + 1, :]
    y = acc + cb_ref[...]
    mu = jnp.mean(y, axis=-1, keepdims=True)
    d = y - mu
    var = jnp.mean(d * d, axis=-1, keepdims=True)
    y = d * lax.rsqrt(var + 1e-5) * lg_ref[...] + lb_ref[...]
    o_ref[0] = y * _sigmoid(y)


def _conv(hglu, halo_src, buf32, cw, cb, lg, lb, tt):
    b, t, c = hglu.shape
    per = max(tt // HALO, 1)
    return pl.pallas_call(
        _conv_kernel,
        grid=(b, t // tt),
        in_specs=[
            pl.BlockSpec((1, tt, c), lambda bi, i: (bi, i, 0)),
            pl.BlockSpec((1, HALO, c), lambda bi, i: (bi, jnp.maximum(i * per - 1, 0), 0)),
            pl.BlockSpec((1, HALO, c), lambda bi, i: (bi, 0, 0)),
            _full(cw.shape), _full(cb.shape), _full(lg.shape), _full(lb.shape),
        ],
        out_specs=pl.BlockSpec((1, tt, c), lambda bi, i: (bi, i, 0)),
        out_shape=jax.ShapeDtypeStruct((b, t, c), F32),
        scratch_shapes=[pltpu.VMEM((HALO + tt, c), F32)],
        compiler_params=_params("parallel", "arbitrary"),
        name="conv",
    )(hglu, halo_src, buf32, cw, cb, lg, lb)


STEP_GROUP = 8


def _wkv_kernel(p_ref, shift_ref, s0_ref, mu_ref, w0_ref, a0_ref, wup_ref, aup_ref, gup_ref,
                kkw_ref, ka_ref, rk_ref, gng_ref, gnb_ref, e_ref, eye_ref,
                o_ref, sout_ref,
                state, pext, r_s, w_s, k_s, kk_s, b_s, v_s, g_s, o_s):
    c = pl.program_id(1)
    nb, tc = p_ref.shape[0], p_ref.shape[1]
    e = e_ref[...]
    w_off = 3 * BRANCH_W

    @pl.when(c == 0)
    def _():
        state[...] = s0_ref[...]

    for b in range(nb):
        @pl.when(c == 0)
        def _():
            pext[b, 7:8, :] = shift_ref[b]

        @pl.when(c > 0)
        def _():
            pext[b, 7:8, :] = pext[b, tc + 7:tc + 8, :]

        p = p_ref[b]
        pext[b, 8:tc + 8, :] = p
        xm = p + (pext[b, 7:tc + 7, :] - p) * mu_ref[...]
        r = xm[:, :BRANCH_W]
        k = xm[:, BRANCH_W:2 * BRANCH_W]
        v = xm[:, 2 * BRANCH_W:w_off]
        wa = xm[:, w_off:w_off + 2 * LORA]
        gd = xm[:, w_off + 2 * LORA:]
        w_raw = w0_ref[...] + jnp.dot(jnp.tanh(wa), wup_ref[...], precision=HIGHEST, preferred_element_type=F32)
        a = _sigmoid(a0_ref[...] + jnp.dot(wa, aup_ref[...], precision=HIGHEST, preferred_element_type=F32))
        g = jnp.dot(_sigmoid(gd), gup_ref[...], precision=HIGHEST, preferred_element_type=F32)
        kk = k * kkw_ref[...]
        kk = kk / jnp.maximum(jnp.sqrt(_seg_sum(kk * kk, e, 3)), 1e-12)
        r_s[b] = r
        w_s[b] = jnp.exp(-math.exp(-0.5) * _sigmoid(w_raw))
        k_s[b] = k * (1.0 + (a - 1.0) * ka_ref[...])
        kk_s[b] = kk
        b_s[b] = kk * a
        v_s[b] = v
        g_s[b] = g

    eye = eye_ref[...]

    def bcast(blocks, j):
        return jnp.concatenate([jnp.broadcast_to(blk[j:j + 1, :], (HEAD, BRANCH_W)) for blk in blocks], axis=0)

    def group(t8, carry):
        base = pl.multiple_of(t8 * STEP_GROUP, STEP_GROUP)
        kk8, v8, w8, b8, k8, r8 = ([ref[b, pl.ds(base, STEP_GROUP), :] for b in range(nb)]
                                   for ref in (kk_s, v_s, w_s, b_s, k_s, r_s))
        s = state[...]
        outs = [[] for _ in range(nb)]
        for j in range(STEP_GROUP):
            sk = _seg_sum(s * bcast(kk8, j), e, 2)
            vb = _seg_sum(bcast(v8, j) * eye, e, 2)
            s = s * bcast(w8, j) - sk * bcast(b8, j) + vb * bcast(k8, j)
            ob = _seg_sum(s * bcast(r8, j), e, 2) * eye
            for b in range(nb):
                outs[b].append(jnp.sum(ob[b * HEAD:(b + 1) * HEAD, :], axis=0, keepdims=True))
        state[...] = s
        for b in range(nb):
            o_s[b, pl.ds(base, STEP_GROUP), :] = jnp.concatenate(outs[b], axis=0)
        return carry

    lax.fori_loop(0, tc // STEP_GROUP, group, 0)

    for b in range(nb):
        o = o_s[b]
        d = o - _seg_sum(o, e, 3) * (1.0 / HEAD)
        var = _seg_sum(d * d, e, 3) * (1.0 / HEAD)
        y = d * lax.rsqrt(var + GN_EPS) * gng_ref[...] + gnb_ref[...]
        bonus = _seg_sum(r_s[b] * k_s[b] * rk_ref[...], e, 3) * v_s[b]
        o_ref[b] = (y + bonus) * g_s[b]

    @pl.when(c == pl.num_programs(1) - 1)
    def _():
        sout_ref[...] = state[...]


def _wkv(p, shift, s0, consts, e256, eye, nb, tc):
    b, t, _ = p.shape
    chunk = lambda w: pltpu.VMEM((nb, tc, w), F32)
    return pl.pallas_call(
        _wkv_kernel,
        grid=(b // nb, t // tc),
        in_specs=[
            pl.BlockSpec((nb, tc, B_COLS), lambda bi, c: (bi, c, 0)),
            pl.BlockSpec((nb, 1, B_COLS), lambda bi, c: (bi, 0, 0)),
            pl.BlockSpec((nb * HEAD, BRANCH_W), lambda bi, c: (bi, 0)),
        ] + [_full(a.shape) for a in consts] + [_full(e256.shape), _full(eye.shape)],
        out_specs=[
            pl.BlockSpec((nb, tc, BRANCH_W), lambda bi, c: (bi, c, 0)),
            pl.BlockSpec((nb * HEAD, BRANCH_W), lambda bi, c: (bi, 0)),
        ],
        out_shape=[jax.ShapeDtypeStruct((b, t, BRANCH_W), F32), jax.ShapeDtypeStruct((b * HEAD, BRANCH_W), F32)],
        scratch_shapes=[pltpu.VMEM((nb * HEAD, BRANCH_W), F32), pltpu.VMEM((nb, tc + 8, B_COLS), F32)]
        + [chunk(BRANCH_W) for _ in range(8)],
        compiler_params=_params("parallel", "arbitrary"),
        name="wkv",
    )(p, shift, s0, *consts, e256, eye)


def _top3(gate, valid, idx):
    sel = jnp.zeros(gate.shape, jnp.bool_)
    avail = valid
    big = float(gate.shape[0])
    for _ in range(MOBA_TOPK):
        g = jnp.where(avail, gate, -jnp.inf)
        top = jnp.max(g, axis=0, keepdims=True)
        first = jnp.min(jnp.where(avail & (g == top), idx, big), axis=0, keepdims=True)
        pick = idx == first
        sel = sel | pick
        avail = avail & jnp.logical_not(pick)
    return sel


def _kmean_kernel(k_ref, o_ref):
    k = k_ref[0, 0]
    nb = k.shape[0] // MOBA_BLOCK
    o_ref[0, 0] = jnp.mean(k.reshape(nb, MOBA_BLOCK, HEAD), axis=1)


def _kmean(k_heads):
    b, h, s, d = k_heads.shape
    nb = s // MOBA_BLOCK
    return pl.pallas_call(
        _kmean_kernel,
        grid=(b, h),
        in_specs=[pl.BlockSpec((1, 1, s, d), lambda bi, hi: (bi, hi, 0, 0))],
        out_specs=pl.BlockSpec((1, 1, nb, d), lambda bi, hi: (bi, hi, 0, 0)),
        out_shape=jax.ShapeDtypeStruct((b, h, nb, d), F32),
        compiler_params=_params("parallel", "parallel"),
        name="kmean",
    )(k_heads)


def _moba_prompt_kernel(q_ref, k_ref, vt_ref, km_ref, slope_ref, o_ref, selb):
    i = pl.program_id(2)
    j_own = i // (MOBA_BLOCK // Q_BLOCK)
    slope = slope_ref[0]
    qf = q_ref[0, 0] * (HEAD ** -0.5)
    q = qf.astype(BF16)
    nb = km_ref.shape[2]

    nt = (((1,), (1,)), ((), ()))
    gate = lax.dot_general(km_ref[0, 0], qf, nt, precision=HIGHEST, preferred_element_type=F32)
    blk = lax.broadcasted_iota(jnp.int32, gate.shape, 0)
    sel = _top3(gate, blk < j_own, blk.astype(F32))
    selb[...] = jnp.where(sel, 0.0, NEG)

    key = lax.broadcasted_iota(jnp.int32, (MOBA_BLOCK, Q_BLOCK), 0)
    t_rel = lax.broadcasted_iota(jnp.int32, (1, Q_BLOCK), 1) + (i % (MOBA_BLOCK // Q_BLOCK)) * Q_BLOCK
    col_bias = slope * key.astype(F32)
    t_bias = slope * t_rel.astype(F32)

    def scores(n):
        start = pl.multiple_of(n * MOBA_BLOCK, MOBA_BLOCK)
        kb = k_ref[0, 0, pl.ds(start, MOBA_BLOCK), :]
        return lax.dot_general(kb, q, nt, preferred_element_type=F32) + col_bias, start

    s, start = scores(j_own)
    s = jnp.where(key <= t_rel, s - t_bias, NEG)
    m = jnp.max(s, axis=0, keepdims=True)
    p = jnp.exp(s - m)
    l = jnp.sum(p, axis=0, keepdims=True)
    acc = jnp.dot(vt_ref[0, 0, :, pl.ds(start, MOBA_BLOCK)], p.astype(BF16), preferred_element_type=F32)

    def body(n, carry):
        m, l, acc = carry
        s, start = scores(n)
        row = selb[pl.ds(n, 1), :] - t_bias - slope * ((j_own - n) * MOBA_BLOCK).astype(F32)
        s = s + row
        m_new = jnp.maximum(m, jnp.max(s, axis=0, keepdims=True))
        alpha = jnp.exp(m - m_new)
        p = jnp.exp(s - m_new)
        l = alpha * l + jnp.sum(p, axis=0, keepdims=True)
        pv = jnp.dot(vt_ref[0, 0, :, pl.ds(start, MOBA_BLOCK)], p.astype(BF16), preferred_element_type=F32)
        return m_new, l, alpha * acc + pv

    m, l, acc = lax.fori_loop(0, j_own, body, (m, l, acc))
    o_ref[0, 0] = acc / l


def _moba_prompt(q_heads, k_heads_bf, vt_heads_bf, kmean, slopes):
    b, h, s, d = q_heads.shape
    nb = s // MOBA_BLOCK
    return pl.pallas_call(
        _moba_prompt_kernel,
        grid=(b, h, s // Q_BLOCK),
        in_specs=[
            pl.BlockSpec((1, 1, Q_BLOCK, d), lambda bi, hi, i: (bi, hi, i, 0)),
            pl.BlockSpec((1, 1, s, d), lambda bi, hi, i: (bi, hi, 0, 0)),
            pl.BlockSpec((1, 1, d, s), lambda bi, hi, i: (bi, hi, 0, 0)),
            pl.BlockSpec((1, 1, nb, d), lambda bi, hi, i: (bi, hi, 0, 0)),
            pl.BlockSpec((1, 1, Q_BLOCK), lambda bi, hi, i: (hi, 0, 0)),
        ],
        out_specs=pl.BlockSpec((1, 1, d, Q_BLOCK), lambda bi, hi, i: (bi, hi, 0, i)),
        out_shape=jax.ShapeDtypeStruct((b, h, d, s), F32),
        scratch_shapes=[pltpu.VMEM((nb, Q_BLOCK), F32)],
        compiler_params=_params("parallel", "parallel", "arbitrary"),
        name="moba_prompt",
    )(q_heads, k_heads_bf, vt_heads_bf, kmean, slopes)


QL = 128


def _moba_pages_kernel(pt_ref, k0_ref, k1_ref, v0_ref, v1_ref, qb_ref, cb_ref, rb_ref, sl_ref, hm_ref,
                       km_ref, m_ref, l_ref, o_ref):
    n = pl.program_id(1)
    t_new = o_ref.shape[2]
    k = jnp.concatenate([k0_ref[0, 0], k1_ref[0, 0]], axis=0)
    v = jnp.concatenate([v0_ref[0, 0], v1_ref[0, 0]], axis=0)
    km_ref[0, 0] = jnp.mean(k, axis=0, keepdims=True)
    s = jnp.dot(k.astype(BF16), qb_ref[0].astype(BF16), preferred_element_type=F32)
    s = s + cb_ref[...] + (rb_ref[...] + sl_ref[...] * (n * MOBA_BLOCK).astype(F32))
    m = jnp.max(s, axis=0, keepdims=True)
    p = jnp.exp(s - m)
    m_ref[0, 0] = m
    l_ref[0, 0] = jnp.sum(p, axis=0, keepdims=True)
    full = jnp.dot(p.T.astype(BF16), v.astype(BF16), preferred_element_type=F32)
    acc = jnp.zeros((t_new, BRANCH_W), F32)
    for h in range(N_HEADS):
        acc = acc + full[h * t_new:(h + 1) * t_new, :] * hm_ref[h:h + 1, :]
    o_ref[0, 0] = acc


def _moba_pages(page_table, ck, cv, layer, qblk, col_bias, row_bias, slope_lane, head_mask, t_new):
    db, n_pages = page_table.shape
    nblk = n_pages // 2
    page = lambda off: pl.BlockSpec((1, 1, PAGE, BRANCH_W), lambda b, n, pt: (layer, pt[b, 2 * n + off], 0, 0))
    stat = lambda w: pl.BlockSpec((1, 1, 1, w), lambda b, n, pt: (b, n, 0, 0))
    return pl.pallas_call(
        _moba_pages_kernel,
        grid_spec=pltpu.PrefetchScalarGridSpec(
            num_scalar_prefetch=1,
            grid=(db, nblk),
            in_specs=[page(0), page(1), page(0), page(1),
                      pl.BlockSpec((1, BRANCH_W, QL), lambda b, n, pt: (b, 0, 0)),
                      pl.BlockSpec(col_bias.shape, lambda b, n, pt: (0, 0)),
                      pl.BlockSpec(row_bias.shape, lambda b, n, pt: (0, 0)),
                      pl.BlockSpec(slope_lane.shape, lambda b, n, pt: (0, 0)),
                      pl.BlockSpec(head_mask.shape, lambda b, n, pt: (0, 0))],
            out_specs=[stat(BRANCH_W), stat(QL), stat(QL),
                       pl.BlockSpec((1, 1, t_new, BRANCH_W), lambda b, n, pt: (b, n, 0, 0))],
        ),
        out_shape=[jax.ShapeDtypeStruct((db, nblk, 1, BRANCH_W), F32),
                   jax.ShapeDtypeStruct((db, nblk, 1, QL), F32),
                   jax.ShapeDtypeStruct((db, nblk, 1, QL), F32),
                   jax.ShapeDtypeStruct((db, nblk, t_new, BRANCH_W), F32)],
        compiler_params=_params("parallel", "arbitrary"),
        name="moba_pages",
    )(page_table, ck, ck, cv, cv, qblk, col_bias, row_bias, slope_lane, head_mask)


def _moba_combine_kernel(km_ref, m_ref, l_ref, op_ref, qb_ref, kn_ref, vn_ref, nb_ref, ex_ref, o_ref):
    t_new = kn_ref.shape[1]
    qb = qb_ref[0]
    gate = jnp.dot(km_ref[0], qb, precision=HIGHEST, preferred_element_type=F32)
    blk = lax.broadcasted_iota(jnp.int32, gate.shape, 0)
    sel = _top3(gate, blk >= 0, blk.astype(F32))
    m_blk = jnp.where(sel, m_ref[0], NEG)
    s_new = jnp.dot(kn_ref[0].astype(BF16), qb.astype(BF16), preferred_element_type=F32) + nb_ref[...]
    m_all = jnp.maximum(jnp.max(m_blk, axis=0, keepdims=True), jnp.max(s_new, axis=0, keepdims=True))
    w_blk = jnp.where(sel, jnp.exp(m_blk - m_all), 0.0)
    p_new = jnp.exp(s_new - m_all)
    denom = jnp.sum(w_blk * l_ref[0], axis=0, keepdims=True) + jnp.sum(p_new, axis=0, keepdims=True)
    vn = vn_ref[0]
    outs = []
    for t in range(t_new):
        ex = ex_ref[t]
        wt = _split_dot(w_blk, ex, 3)
        pt = _split_dot(p_new, ex, 3)
        num = (jnp.sum(wt * op_ref[0, t], axis=0, keepdims=True)
               + jnp.sum(pt * vn, axis=0, keepdims=True))
        outs.append(num / _split_dot(denom, ex, 3))
    o_ref[0] = jnp.concatenate(outs, axis=0)


def _moba_combine(kmean, m, l, o_part, qblk, k_new, v_new, new_bias, expand):
    db, t_new, nblk, _ = o_part.shape
    lead = lambda shape: pl.BlockSpec((1,) + shape, lambda b: (b,) + (0,) * len(shape))
    return pl.pallas_call(
        _moba_combine_kernel,
        grid=(db,),
        in_specs=[lead((nblk, BRANCH_W)), lead((nblk, QL)), lead((nblk, QL)), lead((t_new, nblk, BRANCH_W)),
                  lead((BRANCH_W, QL)), lead((t_new, BRANCH_W)), lead((t_new, BRANCH_W)),
                  _full(new_bias.shape), _full(expand.shape)],
        out_specs=lead((t_new, BRANCH_W)),
        out_shape=jax.ShapeDtypeStruct((db, t_new, BRANCH_W), F32),
        compiler_params=_params("parallel"),
        name="moba_combine",
    )(kmean, m, l, o_part, qblk, k_new, v_new, new_bias, expand)


ROUTE_LANES = 128


def _merge_kernel(x_ref, oa_ref, ob_ref, oc_ref, g1_ref, wg_ref, wb_ref, wo_ref, g2_ref, wr_ref, br_ref,
                  x1_ref, h2_ref, comb_ref):
    x = x_ref[...]
    d = x.shape[1]
    h = (x * lax.rsqrt(jnp.mean(x * x, axis=-1, keepdims=True) + EPS) * g1_ref[...]).astype(BF16)
    merged = None
    for n, br_ref_n in enumerate((oa_ref, ob_ref, oc_ref)):
        gate = _sigmoid(jnp.dot(h, wg_ref[:, n * d:(n + 1) * d], preferred_element_type=F32))
        br = jnp.dot(br_ref_n[...].astype(BF16), wb_ref[n], preferred_element_type=F32)
        merged = gate * br if merged is None else merged + gate * br
    x1 = x + jnp.dot(merged.astype(BF16), wo_ref[...], preferred_element_type=F32)
    x1_ref[...] = x1
    h2 = x1 * lax.rsqrt(jnp.mean(x1 * x1, axis=-1, keepdims=True) + EPS) * g2_ref[...]
    h2_ref[...] = h2.astype(BF16)

    logits = jnp.dot(h2, wr_ref[...], precision=HIGHEST, preferred_element_type=F32) + br_ref[...]
    lane = lax.broadcasted_iota(jnp.int32, logits.shape, 1)
    lane_f = lane.astype(F32)
    is_g = (lane >= N_EXPERTS) & (lane < N_EXPERTS + N_GROUPS)
    glog = jnp.where(is_g, logits, -jnp.inf)
    gmax = jnp.max(glog, axis=-1, keepdims=True)
    gidx = jnp.min(jnp.where(glog == gmax, lane_f - N_EXPERTS, float(N_GROUPS)), axis=-1, keepdims=True)
    gw = 1.0 / jnp.sum(jnp.where(is_g, jnp.exp(logits - gmax), 0.0), axis=-1, keepdims=True)
    in_grp = (lane < N_EXPERTS) & ((lane // PER_GROUP).astype(F32) == gidx)
    big = float(ROUTE_LANES)
    m1 = jnp.where(in_grp, logits, -jnp.inf)
    t1 = jnp.max(m1, axis=-1, keepdims=True)
    i1 = jnp.min(jnp.where(in_grp & (m1 == t1), lane_f, big), axis=-1, keepdims=True)
    rest = in_grp & (lane_f != i1)
    m2 = jnp.where(rest, logits, -jnp.inf)
    t2 = jnp.max(m2, axis=-1, keepdims=True)
    i2 = jnp.min(jnp.where(rest & (m2 == t2), lane_f, big), axis=-1, keepdims=True)
    e2 = jnp.exp(t2 - t1)
    w1 = 1.0 / (1.0 + e2)
    comb_ref[...] = (jnp.where(lane_f == i1, w1, 0.0) + jnp.where(lane_f == i2, e2 * w1, 0.0)) * gw


def _merge(x2, oa, ob, oc, g1, w_gate, w_branch, w_out, g2, w_route, b_route, tm):
    n, d = x2.shape
    row = lambda w: pl.BlockSpec((tm, w), lambda i: (i, 0))
    return pl.pallas_call(
        _merge_kernel,
        grid=(n // tm,),
        in_specs=[row(d), row(BRANCH_W), row(BRANCH_W), row(BRANCH_W), _full(g1.shape), _full(w_gate.shape),
                  _full(w_branch.shape), _full(w_out.shape), _full(g2.shape), _full(w_route.shape),
                  _full(b_route.shape)],
        out_specs=[row(d), row(d), row(ROUTE_LANES)],
        out_shape=[jax.ShapeDtypeStruct((n, d), F32), jax.ShapeDtypeStruct((n, d), BF16),
                   jax.ShapeDtypeStruct((n, ROUTE_LANES), F32)],
        compiler_params=_params("parallel"),
        name="merge",
    )(x2, oa, ob, oc, g1, w_gate, w_branch, w_out, g2, w_route, b_route)


def _moe_kernel(h_ref, comb_ref, x1_ref, wg_ref, wu_ref, wd_ref, o_ref):
    e = pl.program_id(1)

    @pl.when(e == 0)
    def _():
        o_ref[...] = x1_ref[...]

    comb = comb_ref[...]
    lane = lax.broadcasted_iota(jnp.int32, comb.shape, 1)
    c = jnp.sum(jnp.where(lane == e, comb, 0.0), axis=-1, keepdims=True)
    h = h_ref[...]
    hg = jnp.dot(h, wg_ref[0], preferred_element_type=F32)
    hu = jnp.dot(h, wu_ref[0], preferred_element_type=F32)
    act = hg * _sigmoid(hg) * hu * c
    o_ref[...] += jnp.dot(act.astype(BF16), wd_ref[0], preferred_element_type=F32)


def _moe(h2, comb, x1, w_gate, w_up, w_down, tm):
    n, d = x1.shape
    ne, _, f = w_gate.shape
    row = lambda w: pl.BlockSpec((tm, w), lambda i, e: (i, 0))
    return pl.pallas_call(
        _moe_kernel,
        grid=(n // tm, ne),
        in_specs=[row(d), row(ROUTE_LANES), row(d),
                  pl.BlockSpec((1, d, f), lambda i, e: (e, 0, 0)),
                  pl.BlockSpec((1, d, f), lambda i, e: (e, 0, 0)),
                  pl.BlockSpec((1, f, d), lambda i, e: (e, 0, 0))],
        out_specs=row(d),
        out_shape=jax.ShapeDtypeStruct((n, d), F32),
        compiler_params=_params("parallel", "arbitrary"),
        name="moe",
    )(h2, comb, x1, w_gate, w_up, w_down)


def _seg_ones():
    i = jnp.arange(MXU_HALF) // HEAD
    return (i[:, None] == i[None, :]).astype(BF16)


def _layer_consts(prm):
    row = lambda a: a.reshape(1, -1).astype(F32)
    w_in = prm["w_in"]
    off_g = 2 * BRANCH_W + B_COLS + 3 * BRANCH_W
    zeros = jnp.zeros((LORA, BRANCH_W), F32)
    c = dict(
        g1=row(prm["norm1_g"]), g2=row(prm["norm2_g"]),
        w_abc=w_in[:, :off_g].astype(BF16), w_gate=w_in[:, off_g:].astype(BF16),
        qg=row(jnp.tile(prm["q_norm_g"], N_HEADS)), kg=row(jnp.tile(prm["k_norm_g"], N_HEADS)),
        cw=jnp.pad(prm["conv_w"], ((0, 1), (0, 0))), cb=row(prm["conv_b"]),
        lg=row(prm["conv_ln_g"]), lb=row(prm["conv_ln_b"]),
        wkv=[row(prm["rwkv_mu"]), row(prm["rwkv_w0"]), row(prm["rwkv_a0"]),
             jnp.concatenate([prm["rwkv_w_up"], zeros], axis=0), jnp.concatenate([zeros, prm["rwkv_a_up"]], axis=0),
             prm["rwkv_g_up"], row(prm["rwkv_k_k"]), row(prm["rwkv_k_a"]), row(prm["rwkv_r_k"]),
             row(prm["rwkv_gn_g"]), row(prm["rwkv_gn_b"])],
        w_branch=prm["w_branch"].astype(BF16), w_out=prm["w_out"].astype(BF16),
        w_route=jnp.pad(jnp.concatenate([prm["moe_w_expert"], prm["moe_w_group"]], axis=1),
                        ((0, 0), (0, ROUTE_LANES - N_EXPERTS - N_GROUPS))),
        b_route=jnp.pad(jnp.concatenate([prm["moe_b_expert"], prm["moe_b_group"]]),
                        (0, ROUTE_LANES - N_EXPERTS - N_GROUPS)).reshape(1, ROUTE_LANES),
        moe_g=prm["moe_w_gate"].astype(BF16), moe_u=prm["moe_w_up"].astype(BF16),
        moe_d=prm["moe_w_down"].astype(BF16),
    )
    return c


def _state_in(wkv):
    b = wkv.shape[0]
    return jnp.transpose(wkv, (0, 2, 1, 3)).reshape(b * HEAD, BRANCH_W)


def _state_out(s, b):
    return jnp.transpose(s.reshape(b, HEAD, N_HEADS, HEAD), (0, 2, 1, 3))


def _trunk_layer(x, c, conv_buf, shift_prev, wkv_prev, attend, e256, eye, tiles):
    b, t, d = x.shape
    tm, tt, nb, tc, tm_moe = tiles
    x2 = x.reshape(b * t, d)
    hglu, p, q, k, v = _inproj(x2, c["g1"], c["w_abc"], c["qg"], c["kg"], e256, tm)
    hglu = hglu.reshape(b, t, BRANCH_W)
    p = p.reshape(b, t, B_COLS)

    buf32 = jnp.pad(conv_buf, ((0, 0), (HALO - (CONV_WIDTH - 1), 0), (0, 0)))
    o_a = _conv(hglu, hglu if t >= HALO else buf32, buf32, c["cw"], c["cb"], c["lg"], c["lb"], tt)
    conv_new = jnp.concatenate([conv_buf, hglu], axis=1)[:, -(CONV_WIDTH - 1):]

    o_b, s_new = _wkv(p, shift_prev[:, None, :], _state_in(wkv_prev), c["wkv"], e256, eye, nb, tc)
    shift_new = p[:, -1]
    wkv_new = _state_out(s_new, b)

    q3 = q.reshape(b, t, BRANCH_W)
    k3 = k.reshape(b, t, BRANCH_W)
    v3 = v.reshape(b, t, BRANCH_W)
    o_c = attend(q3, k3, v3)

    x1, h2, comb = _merge(x2, o_a.reshape(b * t, BRANCH_W), o_b.reshape(b * t, BRANCH_W),
                          o_c.reshape(b * t, BRANCH_W), c["g1"], c["w_gate"], c["w_branch"], c["w_out"],
                          c["g2"], c["w_route"], c["b_route"], tm)
    x_out = _moe(h2, comb, x1, c["moe_g"], c["moe_u"], c["moe_d"], tm_moe).reshape(b, t, d)
    return (x_out, k3.reshape(b, t, N_HEADS, HEAD), v3.reshape(b, t, N_HEADS, HEAD), conv_new, shift_new, wkv_new)


def _attend_prompt(q3, k3, v3, slopes_q):
    b, s, _ = q3.shape
    heads = lambda a: jnp.transpose(a.reshape(b, s, N_HEADS, HEAD), (0, 2, 1, 3))
    qh, kh = heads(q3), heads(k3)
    vt = jnp.transpose(v3.reshape(b, s, N_HEADS, HEAD), (0, 2, 3, 1)).astype(BF16)
    ot = _moba_prompt(qh, kh.astype(BF16), vt, _kmean(kh), slopes_q)
    return jnp.transpose(ot, (0, 3, 1, 2)).reshape(b, s, BRANCH_W)


def _attend_sample(q3, k3, v3, ck, cv, layer, page_table, sc):
    db, t_new, _ = q3.shape
    q4 = q3.reshape(db, t_new, N_HEADS, HEAD) * (HEAD ** -0.5)
    qblk = jnp.einsum("bthd,hg->bhdgt", q4, jnp.eye(N_HEADS, dtype=F32)).reshape(db, BRANCH_W, N_HEADS * t_new)
    qblk = jnp.pad(qblk, ((0, 0), (0, 0), (0, QL - N_HEADS * t_new)))
    kmean, m, l, o_part = _moba_pages(page_table, ck, cv, layer, qblk, sc["col_bias"], sc["row_bias"],
                                      sc["slope_lane"], sc["head_mask"], t_new)
    nblk = kmean.shape[1]
    return _moba_combine(kmean.reshape(db, nblk, BRANCH_W), m.reshape(db, nblk, QL), l.reshape(db, nblk, QL),
                         jnp.transpose(o_part, (0, 2, 1, 3)), qblk, k3, v3, sc["new_bias"], sc["expand"])


def _sample_consts(t_new, past, slopes):
    lane = jnp.arange(QL)
    used = lane < N_HEADS * t_new
    head = jnp.where(used, lane // t_new, 0)
    step = (lane % t_new).astype(F32)
    slope_lane = jnp.where(used, slopes[head], 0.0).astype(F32)
    key = jnp.arange(MOBA_BLOCK, dtype=F32)
    new_key = jnp.arange(t_new, dtype=F32)
    new_bias = jnp.where(new_key[:, None] <= step[None, :],
                         slope_lane[None, :] * (new_key[:, None] - step[None, :]), NEG)
    out_head = jnp.arange(BRANCH_W) // HEAD
    expand = jnp.stack([(used & (lane % t_new == t))[:, None] & (head[:, None] == out_head[None, :])
                        for t in range(t_new)]).astype(BF16)
    return dict(
        col_bias=slope_lane[None, :] * key[:, None],
        row_bias=(-slope_lane * (past + step))[None, :],
        slope_lane=slope_lane[None, :],
        head_mask=(jnp.arange(N_HEADS)[:, None] == out_head[None, :]).astype(F32),
        new_bias=new_bias.astype(F32),
        expand=expand,
    )


def kernel(x_prompt, x_sample, cache_k, cache_v, page_table, state_conv, state_shift, state_wkv, norm1_g, w_in, conv_w, conv_b, conv_ln_g, conv_ln_b, rwkv_mu, rwkv_w0, rwkv_w_up, rwkv_a0, rwkv_a_up, rwkv_g_up, rwkv_k_k, rwkv_k_a, rwkv_r_k, rwkv_gn_g, rwkv_gn_b, q_norm_g, k_norm_g, w_branch, w_out, norm2_g, moe_w_group, moe_b_group, moe_w_expert, moe_b_expert, moe_w_gate, moe_w_up, moe_w_down):
    stacked = dict(norm1_g=norm1_g, w_in=w_in, conv_w=conv_w, conv_b=conv_b, conv_ln_g=conv_ln_g,
                   conv_ln_b=conv_ln_b, rwkv_mu=rwkv_mu, rwkv_w0=rwkv_w0, rwkv_w_up=rwkv_w_up, rwkv_a0=rwkv_a0,
                   rwkv_a_up=rwkv_a_up, rwkv_g_up=rwkv_g_up, rwkv_k_k=rwkv_k_k, rwkv_k_a=rwkv_k_a,
                   rwkv_r_k=rwkv_r_k, rwkv_gn_g=rwkv_gn_g, rwkv_gn_b=rwkv_gn_b, q_norm_g=q_norm_g,
                   k_norm_g=k_norm_g, w_branch=w_branch, w_out=w_out, norm2_g=norm2_g, moe_w_group=moe_w_group,
                   moe_b_group=moe_b_group, moe_w_expert=moe_w_expert, moe_b_expert=moe_b_expert,
                   moe_w_gate=moe_w_gate, moe_w_up=moe_w_up, moe_w_down=moe_w_down)
    depth = w_in.shape[0]
    b, s, d = x_prompt.shape
    db, t_new, _ = x_sample.shape
    n_pool = cache_k.shape[1]
    past = page_table.shape[1] * PAGE

    slopes = jnp.exp2(-8.0 * (jnp.arange(N_HEADS, dtype=F32) + 1.0) / N_HEADS)
    slopes_q = jnp.broadcast_to(slopes[:, None, None], (N_HEADS, 1, Q_BLOCK))
    e256 = _seg_ones()
    sc = _sample_consts(t_new, past, slopes)
    eye = lambda nb: jnp.tile((jnp.arange(HEAD)[:, None] == (jnp.arange(BRANCH_W) % HEAD)[None, :]).astype(F32), (nb, 1))
    ck = cache_k.reshape(depth, n_pool, PAGE, BRANCH_W)
    cv = cache_v.reshape(depth, n_pool, PAGE, BRANCH_W)

    tiles_p = (512, 512, b, 256, 1024)
    tiles_s = (db * t_new, t_new, 2, t_new, db * t_new)
    xp, xs = x_prompt, x_sample
    outs = [[] for _ in range(10)]
    for l in range(depth):
        c = _layer_consts({k: v[l] for k, v in stacked.items()})
        xp, kp, vp, cp, sp, wp = _trunk_layer(
            xp, c, jnp.zeros((b, CONV_WIDTH - 1, BRANCH_W), F32), jnp.zeros((b, B_COLS), F32),
            jnp.zeros((b, N_HEADS, HEAD, HEAD), F32), functools.partial(_attend_prompt, slopes_q=slopes_q),
            e256, eye(b), tiles_p)
        attend_s = functools.partial(_attend_sample, ck=ck, cv=cv, layer=l, page_table=page_table, sc=sc)
        xs, ks, vs, cs, ss, ws = _trunk_layer(xs, c, state_conv[l], state_shift[l], state_wkv[l], attend_s,
                                              e256, eye(2), tiles_s)
        new = (kp.reshape(b, s // PAGE, PAGE, N_HEADS, HEAD), vp.reshape(b, s // PAGE, PAGE, N_HEADS, HEAD),
               ks, vs, cp, cs, sp, ss, wp, ws)
        for acc, val in zip(outs, new):
            acc.append(val)
    return (xp, xs) + tuple(jnp.stack(a) for a in outs)
```

```python
import functools
import math

import jax
import jax.numpy as jnp
from jax import lax
from jax.experimental import pallas as pl
from jax.experimental.pallas import tpu as pltpu

F32 = jnp.float32
BF16 = jnp.bfloat16
HIGHEST = lax.Precision.HIGHEST

HEAD = 64
N_HEADS = 8
BRANCH_W = N_HEADS * HEAD
CONV_WIDTH = 31
LORA = 64
LORA_G = 128
B_COLS = 3 * BRANCH_W + 2 * LORA + LORA_G
MOBA_BLOCK = 256
MOBA_TOPK = 3
Q_BLOCK = 128
PAGE = 128
N_GROUPS = 4
PER_GROUP = 4
N_EXPERTS = N_GROUPS * PER_GROUP
EPS = 1e-6
GN_EPS = 64e-5
NEG = -1e30
MXU_HALF = 256
VMEM_LIMIT = 56 * 1024 * 1024


def _sigmoid(x):
    return 1.0 / (1.0 + jnp.exp(-x))


def _split_dot(x, e, passes):
    acc = None
    rem = x
    for i in range(passes):
        hi = rem.astype(BF16)
        d = jnp.dot(hi, e, preferred_element_type=F32)
        acc = d if acc is None else acc + d
        if i + 1 < passes:
            rem = rem - hi.astype(F32)
    return acc


def _seg_sum(x, e256, passes):
    parts = [_split_dot(x[:, i:i + MXU_HALF], e256, passes) for i in range(0, x.shape[1], MXU_HALF)]
    return jnp.concatenate(parts, axis=1)


def _params(*sem):
    return pltpu.CompilerParams(dimension_semantics=sem, vmem_limit_bytes=VMEM_LIMIT)


def _full(shape):
    n = len(shape)
    return pl.BlockSpec(shape, lambda *_: (0,) * n)


def _inproj_kernel(x_ref, g_ref, w_ref, qg_ref, kg_ref, e_ref, hglu_ref, p_ref, q_ref, k_ref, v_ref):
    x = x_ref[...]
    h = (x * lax.rsqrt(jnp.mean(x * x, axis=-1, keepdims=True) + EPS) * g_ref[...]).astype(BF16)
    c0, c1, c2 = 2 * BRANCH_W, 2 * BRANCH_W + B_COLS, 2 * BRANCH_W + B_COLS + 3 * BRANCH_W
    u = jnp.dot(h, w_ref[:, :c0], preferred_element_type=F32)
    hglu_ref[...] = u[:, :BRANCH_W] * _sigmoid(u[:, BRANCH_W:])
    p_ref[...] = jnp.dot(h, w_ref[:, c0:c1], preferred_element_type=F32)
    qkv = jnp.dot(h, w_ref[:, c1:c2], preferred_element_type=F32)
    e = e_ref[...]
    q = qkv[:, :BRANCH_W]
    k = qkv[:, BRANCH_W:2 * BRANCH_W]
    q_ref[...] = q * lax.rsqrt(_seg_sum(q * q, e, 2) * (1.0 / HEAD) + EPS) * qg_ref[...]
    k_ref[...] = k * lax.rsqrt(_seg_sum(k * k, e, 2) * (1.0 / HEAD) + EPS) * kg_ref[...]
    v_ref[...] = qkv[:, 2 * BRANCH_W:]


def _inproj(x2, g1, w_abc, qg, kg, e256, tm):
    n, d = x2.shape
    row = lambda w: pl.BlockSpec((tm, w), lambda i: (i, 0))
    outs = [BRANCH_W, B_COLS, BRANCH_W, BRANCH_W, BRANCH_W]
    return pl.pallas_call(
        _inproj_kernel,
        grid=(n // tm,),
        in_specs=[row(d), _full(g1.shape), _full(w_abc.shape), _full(qg.shape), _full(kg.shape), _full(e256.shape)],
        out_specs=[row(w) for w in outs],
        out_shape=[jax.ShapeDtypeStruct((n, w), F32) for w in outs],
        compiler_params=_params("parallel"),
        name="inproj",
    )(x2, g1, w_abc, qg, kg, e256)


HALO = 32


def _conv_kernel(h_ref, halo_ref, buf_ref, cw_ref, cb_ref, lg_ref, lb_ref, o_ref, ext):
    i = pl.program_id(1)
    tt = h_ref.shape[1]

    @pl.when(i == 0)
    def _():
        ext[0:HALO, :] = buf_ref[0]

    @pl.when(i > 0)
    def _():
        ext[0:HALO, :] = halo_ref[0]

    ext[HALO:, :] = h_ref[0]
    first = HALO - (CONV_WIDTH - 1)
    acc = ext[first:first + tt, :] * cw_ref[0:1, :]
    for w in range(1, CONV_WIDTH):
        acc = acc + ext[first + w:first + w + tt, :] * cw_ref[w:w + 1, :]
    y = acc + cb_ref[...]
    mu = jnp.mean(y, axis=-1, keepdims=True)
    d = y - mu
    var = jnp.mean(d * d, axis=-1, keepdims=True)
    y = d * lax.rsqrt(var + 1e-5) * lg_ref[...] + lb_ref[...]
    o_ref[0] = y * _sigmoid(y)


def _conv(hglu, halo_src, buf32, cw, cb, lg, lb, tt):
    b, t, c = hglu.shape
    per = max(tt // HALO, 1)
    return pl.pallas_call(
        _conv_kernel,
        grid=(b, t // tt),
        in_specs=[
            pl.BlockSpec((1, tt, c), lambda bi, i: (bi, i, 0)),
            pl.BlockSpec((1, HALO, c), lambda bi, i: (bi, jnp.maximum(i * per - 1, 0), 0)),
            pl.BlockSpec((1, HALO, c), lambda bi, i: (bi, 0, 0)),
            _full(cw.shape), _full(cb.shape), _full(lg.shape), _full(lb.shape),
        ],
        out_specs=pl.BlockSpec((1, tt, c), lambda bi, i: (bi, i, 0)),
        out_shape=jax.ShapeDtypeStruct((b, t, c), F32),
        scratch_shapes=[pltpu.VMEM((HALO + tt, c), F32)],
        compiler_params=_params("parallel", "arbitrary"),
        name="conv",
    )(hglu, halo_src, buf32, cw, cb, lg, lb)


STEP_GROUP = 8


def _wkv_kernel(p_ref, shift_ref, s0_ref, mu_ref, w0_ref, a0_ref, wup_ref, aup_ref, gup_ref,
                kkw_ref, ka_ref, rk_ref, gng_ref, gnb_ref, e_ref, eye_ref,
                o_ref, sout_ref,
                state, pext, r_s, w_s, k_s, kk_s, b_s, v_s, g_s, o_s):
    c = pl.program_id(1)
    nb, tc = p_ref.shape[0], p_ref.shape[1]
    e = e_ref[...]
    w_off = 3 * BRANCH_W

    @pl.when(c == 0)
    def _():
        state[...] = s0_ref[...]

    for b in range(nb):
        @pl.when(c == 0)
        def _():
            pext[b, 7:8, :] = shift_ref[b]

        @pl.when(c > 0)
        def _():
            pext[b, 7:8, :] = pext[b, tc + 7:tc + 8, :]

        p = p_ref[b]
        pext[b, 8:tc + 8, :] = p
        xm = p + (pext[b, 7:tc + 7, :] - p) * mu_ref[...]
        r = xm[:, :BRANCH_W]
        k = xm[:, BRANCH_W:2 * BRANCH_W]
        v = xm[:, 2 * BRANCH_W:w_off]
        wa = xm[:, w_off:w_off + 2 * LORA]
        gd = xm[:, w_off + 2 * LORA:]
        w_raw = w0_ref[...] + jnp.dot(jnp.tanh(wa), wup_ref[...], precision=HIGHEST, preferred_element_type=F32)
        a = _sigmoid(a0_ref[...] + jnp.dot(wa, aup_ref[...], precision=HIGHEST, preferred_element_type=F32))
        g = jnp.dot(_sigmoid(gd), gup_ref[...], precision=HIGHEST, preferred_element_type=F32)
        kk = k * kkw_ref[...]
        kk = kk / jnp.maximum(jnp.sqrt(_seg_sum(kk * kk, e, 3)), 1e-12)
        r_s[b] = r
        w_s[b] = jnp.exp(-math.exp(-0.5) * _sigmoid(w_raw))
        k_s[b] = k * (1.0 + (a - 1.0) * ka_ref[...])
        kk_s[b] = kk
        b_s[b] = kk * a
        v_s[b] = v
        g_s[b] = g

    eye = eye_ref[...]
    e2 = jnp.concatenate([e, e], axis=0)

    def bc(block, j):
        return jnp.broadcast_to(block[j:j + 1, :], (HEAD, BRANCH_W))

    def halves(hi, lo):
        return [jnp.concatenate([hi[:, i:i + MXU_HALF], lo[:, i:i + MXU_HALF]], axis=1)
                for i in range(0, BRANCH_W, MXU_HALF)]

    def unhalve(res, j, n):
        return jnp.concatenate([res[j * HEAD:(j + 1) * HEAD], res[(n + j) * HEAD:(n + j + 1) * HEAD]], axis=1)

    def group(t8, carry):
        base = pl.multiple_of(t8 * STEP_GROUP, STEP_GROUP)
        bs = range(nb)
        kk8, v8, w8, b8, k8, r8 = ([ref[b, pl.ds(base, STEP_GROUP), :] for b in bs]
                                   for ref in (kk_s, v_s, w_s, b_s, k_s, r_s))
        vk = []
        for b in bs:
            v_hi = v8[b].astype(BF16).astype(F32)
            v_lo = v8[b] - v_hi
            hi = jnp.concatenate([(bc(v_hi, j) * eye).astype(BF16) for j in range(STEP_GROUP)], axis=0)
            lo = jnp.concatenate([(bc(v_lo, j) * eye).astype(BF16) for j in range(STEP_GROUP)], axis=0)
            res = jnp.dot(jnp.concatenate(halves(hi, lo), axis=0), e2, preferred_element_type=F32)
            vk.append([unhalve(res, j, STEP_GROUP) * bc(k8[b], j) for j in range(STEP_GROUP)])
        s = [state[b * HEAD:(b + 1) * HEAD, :] for b in bs]
        outs = [[] for _ in bs]
        stack = lambda a: [a[:, :MXU_HALF], a[:, MXU_HALF:]]
        out_row = lambda ro: jnp.sum(ro * eye, axis=0, keepdims=True)
        ys = [None for _ in bs]
        for j in range(STEP_GROUP):
            lhs = []
            for b in bs:
                x = s[b] * bc(kk8[b], j)
                x_hi = x.astype(BF16)
                parts = stack(x_hi) + stack((x - x_hi.astype(F32)).astype(BF16))
                lhs.append(jnp.concatenate(parts + (stack(ys[b]) if j else []), axis=0))
            res = [jnp.dot(lhs[b], e, preferred_element_type=F32) for b in bs]
            for b in bs:
                sk = unhalve(res[b], 0, 1) + unhalve(res[b], 2, 1)
                if j:
                    outs[b].append(out_row(unhalve(res[b], 4, 1)))
                s[b] = s[b] * bc(w8[b], j) - sk * bc(b8[b], j) + vk[b][j]
                ys[b] = (s[b] * bc(r8[b], j)).astype(BF16)
        last = jnp.dot(jnp.concatenate([h for b in bs for h in stack(ys[b])], axis=0), e, preferred_element_type=F32)
        for b in bs:
            outs[b].append(out_row(unhalve(last, 2 * b, 1)))
            state[b * HEAD:(b + 1) * HEAD, :] = s[b]
            o_s[b, pl.ds(base, STEP_GROUP), :] = jnp.concatenate(outs[b], axis=0)
        return carry

    lax.fori_loop(0, tc // STEP_GROUP, group, 0)

    for b in range(nb):
        o = o_s[b]
        d = o - _seg_sum(o, e, 3) * (1.0 / HEAD)
        var = _seg_sum(d * d, e, 3) * (1.0 / HEAD)
        y = d * lax.rsqrt(var + GN_EPS) * gng_ref[...] + gnb_ref[...]
        bonus = _seg_sum(r_s[b] * k_s[b] * rk_ref[...], e, 3) * v_s[b]
        o_ref[b] = (y + bonus) * g_s[b]

    @pl.when(c == pl.num_programs(1) - 1)
    def _():
        sout_ref[...] = state[...]


def _wkv(p, shift, s0, consts, e256, eye, nb, tc):
    b, t, _ = p.shape
    chunk = lambda w: pltpu.VMEM((nb, tc, w), F32)
    return pl.pallas_call(
        _wkv_kernel,
        grid=(b // nb, t // tc),
        in_specs=[
            pl.BlockSpec((nb, tc, B_COLS), lambda bi, c: (bi, c, 0)),
            pl.BlockSpec((nb, 1, B_COLS), lambda bi, c: (bi, 0, 0)),
            pl.BlockSpec((nb * HEAD, BRANCH_W), lambda bi, c: (bi, 0)),
        ] + [_full(a.shape) for a in consts] + [_full(e256.shape), _full(eye.shape)],
        out_specs=[
            pl.BlockSpec((nb, tc, BRANCH_W), lambda bi, c: (bi, c, 0)),
            pl.BlockSpec((nb * HEAD, BRANCH_W), lambda bi, c: (bi, 0)),
        ],
        out_shape=[jax.ShapeDtypeStruct((b, t, BRANCH_W), F32), jax.ShapeDtypeStruct((b * HEAD, BRANCH_W), F32)],
        scratch_shapes=[pltpu.VMEM((nb * HEAD, BRANCH_W), F32), pltpu.VMEM((nb, tc + 8, B_COLS), F32)]
        + [chunk(BRANCH_W) for _ in range(8)],
        compiler_params=_params("parallel", "arbitrary"),
        name="wkv",
    )(p, shift, s0, *consts, e256, eye)


def _top3(gate, valid, idx):
    sel = jnp.zeros(gate.shape, jnp.bool_)
    avail = valid
    big = float(gate.shape[0])
    for _ in range(MOBA_TOPK):
        g = jnp.where(avail, gate, -jnp.inf)
        top = jnp.max(g, axis=0, keepdims=True)
        first = jnp.min(jnp.where(avail & (g == top), idx, big), axis=0, keepdims=True)
        pick = idx == first
        sel = sel | pick
        avail = avail & jnp.logical_not(pick)
    return sel


def _kmean_kernel(k_ref, o_ref):
    k = k_ref[0, 0]
    nb = k.shape[0] // MOBA_BLOCK
    o_ref[0, 0] = jnp.mean(k.reshape(nb, MOBA_BLOCK, HEAD), axis=1)


def _kmean(k_heads):
    b, h, s, d = k_heads.shape
    nb = s // MOBA_BLOCK
    return pl.pallas_call(
        _kmean_kernel,
        grid=(b, h),
        in_specs=[pl.BlockSpec((1, 1, s, d), lambda bi, hi: (bi, hi, 0, 0))],
        out_specs=pl.BlockSpec((1, 1, nb, d), lambda bi, hi: (bi, hi, 0, 0)),
        out_shape=jax.ShapeDtypeStruct((b, h, nb, d), F32),
        compiler_params=_params("parallel", "parallel"),
        name="kmean",
    )(k_heads)


HEAD_GROUP = 4
LOG2E = 1.4426950408889634


def _moba_prompt_kernel(q_ref, k_ref, vt_ref, km_ref, slope_ref, o_ref, selb):
    j_own = pl.program_id(2)
    hg = q_ref.shape[1]
    nt = (((1,), (1,)), ((), ()))
    key = lax.broadcasted_iota(jnp.int32, (MOBA_BLOCK, MOBA_BLOCK), 0)
    qry = lax.broadcasted_iota(jnp.int32, (MOBA_BLOCK, MOBA_BLOCK), 1)
    rel = (key - qry).astype(F32)
    causal = key <= qry

    hs = range(hg)
    slope2 = [slope_ref[h] * LOG2E for h in hs]
    qf = [q_ref[0, h] * (HEAD ** -0.5) for h in hs]
    qs = [(qf[h] * LOG2E).astype(BF16) for h in hs]
    col_bias = [slope2[h] * key[:, 0:1].astype(F32) for h in hs]

    def qk(h, start):
        return lax.dot_general(k_ref[0, h, pl.ds(start, MOBA_BLOCK), :], qs[h], nt, preferred_element_type=F32)

    def pv(h, start, p):
        return jnp.dot(vt_ref[0, h, :, pl.ds(start, MOBA_BLOCK)], p.astype(BF16), preferred_element_type=F32)

    own = pl.multiple_of(j_own * MOBA_BLOCK, MOBA_BLOCK)
    s_own = [qk(h, own) for h in hs]
    for h in hs:
        gate = lax.dot_general(km_ref[0, h], qf[h], nt, precision=HIGHEST, preferred_element_type=F32)
        blk = lax.broadcasted_iota(jnp.int32, gate.shape, 0)
        sel = _top3(gate, blk < j_own, blk.astype(F32))
        dist = ((blk - j_own) * MOBA_BLOCK - qry[0:1, :]).astype(F32)
        selb[h] = jnp.where(sel, slope2[h] * dist, NEG)
    ms, ls, ps = [], [], []
    for h in hs:
        s = jnp.where(causal, s_own[h] + slope2[h] * rel, NEG)
        m = jnp.max(s, axis=0, keepdims=True)
        p = jnp.exp2(s - m)
        ms.append(m)
        ls.append(jnp.sum(p, axis=0, keepdims=True))
        ps.append(p)
    accs = [pv(h, own, ps[h]) for h in hs]

    def body(n, carry):
        ms, ls, accs = carry
        start = pl.multiple_of(n * MOBA_BLOCK, MOBA_BLOCK)
        ss = [qk(h, start) for h in hs]
        m_out, l_out, alphas, ps = [], [], [], []
        for h in hs:
            s = ss[h] + col_bias[h] + selb[h, pl.ds(n, 1), :]
            m_new = jnp.maximum(ms[h], jnp.max(s, axis=0, keepdims=True))
            alpha = jnp.exp2(ms[h] - m_new)
            p = jnp.exp2(s - m_new)
            m_out.append(m_new)
            l_out.append(alpha * ls[h] + jnp.sum(p, axis=0, keepdims=True))
            alphas.append(alpha)
            ps.append(p)
        acc_out = [alphas[h] * accs[h] + pv(h, start, ps[h]) for h in hs]
        return tuple(m_out), tuple(l_out), tuple(acc_out)

    ms, ls, accs = lax.fori_loop(0, j_own, body, (tuple(ms), tuple(ls), tuple(accs)))
    for h in hs:
        o_ref[0, h] = accs[h] / ls[h]


def _moba_prompt(q_heads, k_heads_bf, vt_heads_bf, kmean, slopes):
    b, h, s, d = q_heads.shape
    nb = s // MOBA_BLOCK
    hg = HEAD_GROUP
    return pl.pallas_call(
        _moba_prompt_kernel,
        grid=(b, h // hg, nb),
        in_specs=[
            pl.BlockSpec((1, hg, MOBA_BLOCK, d), lambda bi, hi, i: (bi, hi, i, 0)),
            pl.BlockSpec((1, hg, s, d), lambda bi, hi, i: (bi, hi, 0, 0)),
            pl.BlockSpec((1, hg, d, s), lambda bi, hi, i: (bi, hi, 0, 0)),
            pl.BlockSpec((1, hg, nb, d), lambda bi, hi, i: (bi, hi, 0, 0)),
            pl.BlockSpec((hg, 1, MOBA_BLOCK), lambda bi, hi, i: (hi, 0, 0)),
        ],
        out_specs=pl.BlockSpec((1, hg, d, MOBA_BLOCK), lambda bi, hi, i: (bi, hi, 0, i)),
        out_shape=jax.ShapeDtypeStruct((b, h, d, s), F32),
        scratch_shapes=[pltpu.VMEM((hg, nb, MOBA_BLOCK), F32)],
        compiler_params=_params("parallel", "parallel", "arbitrary"),
        name="moba_prompt",
    )(q_heads, k_heads_bf, vt_heads_bf, kmean, slopes)


QL = 128


BLOCKS_PER_STEP = 2
PAGES_PER_BLOCK = MOBA_BLOCK // PAGE


def _moba_pages_kernel(pt_ref, *refs):
    npg = BLOCKS_PER_STEP * PAGES_PER_BLOCK
    k_refs, v_refs = refs[:npg], refs[npg:2 * npg]
    qt_ref, tab_ref, rowc_ref, slc_ref, km_ref, o_ref, m_ref, l_ref = refs[2 * npg:]
    n = pl.program_id(1)
    rows_per_page = PAGE * N_HEADS

    @pl.when(n == 0)
    def _():
        m_ref[...] = jnp.zeros(m_ref.shape, F32)
        l_ref[...] = jnp.ones(l_ref.shape, F32)

    q = qt_ref[0].astype(BF16)
    lane = lax.broadcasted_iota(jnp.int32, m_ref.shape[1:], 1)
    nt = (((1,), (1,)), ((), ()))
    us = range(BLOCKS_PER_STEP)
    pages = [range(u * PAGES_PER_BLOCK, (u + 1) * PAGES_PER_BLOCK) for u in us]
    rows = lambda refs, u: jnp.concatenate([refs[i][0, 0].reshape(rows_per_page, HEAD) for i in pages[u]], axis=0)
    ss = [lax.dot_general(q, rows(k_refs, u).astype(BF16), nt, preferred_element_type=F32) for u in us]
    ps = []
    for u in us:
        blk = n * BLOCKS_PER_STEP + u
        km_ref[0, u] = sum(jnp.sum(k_refs[i][0, 0], axis=0) for i in pages[u]) * (1.0 / MOBA_BLOCK)
        s = ss[u] + tab_ref[...] + (rowc_ref[:, 0:1] + slc_ref[:, 0:1] * (blk * MOBA_BLOCK).astype(F32))
        m = jnp.max(s, axis=1, keepdims=True)
        p = jnp.exp(s - m)
        m_ref[0] = jnp.where(lane == blk, m, m_ref[0])
        l_ref[0] = jnp.where(lane == blk, jnp.sum(p, axis=1, keepdims=True), l_ref[0])
        ps.append(p.astype(BF16))
    for u in us:
        o_ref[0, u] = jnp.dot(ps[u], rows(v_refs, u).astype(BF16), preferred_element_type=F32)


def _moba_pages(page_table, cache_k, cache_v, layer, qt, table, rowc, slc):
    db, n_pages = page_table.shape
    nblk = n_pages // PAGES_PER_BLOCK
    npg = BLOCKS_PER_STEP * PAGES_PER_BLOCK
    rows = qt.shape[1]
    page = lambda off: pl.BlockSpec((1, 1, PAGE, N_HEADS, HEAD),
                                    lambda b, n, pt: (layer, pt[b, npg * n + off], 0, 0, 0))
    const = lambda a: pl.BlockSpec(a.shape, lambda b, n, pt: (0, 0))
    stat = pl.BlockSpec((1, rows, QL), lambda b, n, pt: (b, 0, 0))
    return pl.pallas_call(
        _moba_pages_kernel,
        grid_spec=pltpu.PrefetchScalarGridSpec(
            num_scalar_prefetch=1,
            grid=(db, nblk // BLOCKS_PER_STEP),
            in_specs=[page(i) for i in range(npg)] * 2
            + [pl.BlockSpec((1, rows, HEAD), lambda b, n, pt: (b, 0, 0)), const(table), const(rowc), const(slc)],
            out_specs=[pl.BlockSpec((1, BLOCKS_PER_STEP, N_HEADS, HEAD), lambda b, n, pt: (b, n, 0, 0)),
                       pl.BlockSpec((1, BLOCKS_PER_STEP, rows, HEAD), lambda b, n, pt: (b, n, 0, 0)),
                       stat, stat],
        ),
        out_shape=[jax.ShapeDtypeStruct((db, nblk, N_HEADS, HEAD), F32),
                   jax.ShapeDtypeStruct((db, nblk, rows, HEAD), F32),
                   jax.ShapeDtypeStruct((db, rows, QL), F32),
                   jax.ShapeDtypeStruct((db, rows, QL), F32)],
        compiler_params=_params("parallel", "arbitrary"),
        name="moba_pages",
    )(page_table, *([cache_k] * npg), *([cache_v] * npg), qt, table, rowc, slc)


def _moba_combine_kernel(km_ref, m_ref, l_ref, op_ref, qb_ref, kn_ref, vn_ref, nb_ref, ex_ref, o_ref):
    t_new = kn_ref.shape[1]
    qb = qb_ref[0]
    gate = jnp.dot(km_ref[0], qb, precision=HIGHEST, preferred_element_type=F32)
    blk = lax.broadcasted_iota(jnp.int32, gate.shape, 0)
    sel = _top3(gate, blk >= 0, blk.astype(F32))
    m_blk = jnp.where(sel, m_ref[0], NEG)
    s_new = jnp.dot(kn_ref[0].astype(BF16), qb.astype(BF16), preferred_element_type=F32) + nb_ref[...]
    m_all = jnp.maximum(jnp.max(m_blk, axis=0, keepdims=True), jnp.max(s_new, axis=0, keepdims=True))
    w_blk = jnp.where(sel, jnp.exp(m_blk - m_all), 0.0)
    p_new = jnp.exp(s_new - m_all)
    denom = jnp.sum(w_blk * l_ref[0], axis=0, keepdims=True) + jnp.sum(p_new, axis=0, keepdims=True)
    vn = vn_ref[0]
    outs = []
    for t in range(t_new):
        ex = ex_ref[t]
        wt = _split_dot(w_blk, ex, 3)
        pt = _split_dot(p_new, ex, 3)
        num = (jnp.sum(wt * op_ref[0, t], axis=0, keepdims=True)
               + jnp.sum(pt * vn, axis=0, keepdims=True))
        outs.append(num / _split_dot(denom, ex, 3))
    o_ref[0] = jnp.concatenate(outs, axis=0)


def _moba_combine(kmean, m, l, o_part, qblk, k_new, v_new, new_bias, expand):
    db, t_new, nblk, _ = o_part.shape
    lead = lambda shape: pl.BlockSpec((1,) + shape, lambda b: (b,) + (0,) * len(shape))
    return pl.pallas_call(
        _moba_combine_kernel,
        grid=(db,),
        in_specs=[lead((nblk, BRANCH_W)), lead((nblk, QL)), lead((nblk, QL)), lead((t_new, nblk, BRANCH_W)),
                  lead((BRANCH_W, QL)), lead((t_new, BRANCH_W)), lead((t_new, BRANCH_W)),
                  _full(new_bias.shape), _full(expand.shape)],
        out_specs=lead((t_new, BRANCH_W)),
        out_shape=jax.ShapeDtypeStruct((db, t_new, BRANCH_W), F32),
        compiler_params=_params("parallel"),
        name="moba_combine",
    )(kmean, m, l, o_part, qblk, k_new, v_new, new_bias, expand)


ROUTE_LANES = 128


def _merge_kernel(x_ref, oa_ref, ob_ref, oc_ref, g1_ref, wg_ref, wb_ref, wo_ref, g2_ref, wr_ref, br_ref,
                  x1_ref, h2_ref, comb_ref):
    x = x_ref[...]
    d = x.shape[1]
    h = (x * lax.rsqrt(jnp.mean(x * x, axis=-1, keepdims=True) + EPS) * g1_ref[...]).astype(BF16)
    merged = None
    for n, br_ref_n in enumerate((oa_ref, ob_ref, oc_ref)):
        gate = _sigmoid(jnp.dot(h, wg_ref[:, n * d:(n + 1) * d], preferred_element_type=F32))
        br = jnp.dot(br_ref_n[...].astype(BF16), wb_ref[n], preferred_element_type=F32)
        merged = gate * br if merged is None else merged + gate * br
    x1 = x + jnp.dot(merged.astype(BF16), wo_ref[...], preferred_element_type=F32)
    x1_ref[...] = x1
    h2 = x1 * lax.rsqrt(jnp.mean(x1 * x1, axis=-1, keepdims=True) + EPS) * g2_ref[...]
    h2_ref[...] = h2.astype(BF16)

    logits = jnp.dot(h2, wr_ref[...], precision=HIGHEST, preferred_element_type=F32) + br_ref[...]
    lane = lax.broadcasted_iota(jnp.int32, logits.shape, 1)
    lane_f = lane.astype(F32)
    is_g = (lane >= N_EXPERTS) & (lane < N_EXPERTS + N_GROUPS)
    glog = jnp.where(is_g, logits, -jnp.inf)
    gmax = jnp.max(glog, axis=-1, keepdims=True)
    gidx = jnp.min(jnp.where(glog == gmax, lane_f - N_EXPERTS, float(N_GROUPS)), axis=-1, keepdims=True)
    gw = 1.0 / jnp.sum(jnp.where(is_g, jnp.exp(logits - gmax), 0.0), axis=-1, keepdims=True)
    in_grp = (lane < N_EXPERTS) & ((lane // PER_GROUP).astype(F32) == gidx)
    big = float(ROUTE_LANES)
    m1 = jnp.where(in_grp, logits, -jnp.inf)
    t1 = jnp.max(m1, axis=-1, keepdims=True)
    i1 = jnp.min(jnp.where(in_grp & (m1 == t1), lane_f, big), axis=-1, keepdims=True)
    rest = in_grp & (lane_f != i1)
    m2 = jnp.where(rest, logits, -jnp.inf)
    t2 = jnp.max(m2, axis=-1, keepdims=True)
    i2 = jnp.min(jnp.where(rest & (m2 == t2), lane_f, big), axis=-1, keepdims=True)
    e2 = jnp.exp(t2 - t1)
    w1 = 1.0 / (1.0 + e2)
    comb_ref[...] = (jnp.where(lane_f == i1, w1, 0.0) + jnp.where(lane_f == i2, e2 * w1, 0.0)) * gw


def _merge(x2, oa, ob, oc, g1, w_gate, w_branch, w_out, g2, w_route, b_route, tm):
    n, d = x2.shape
    row = lambda w: pl.BlockSpec((tm, w), lambda i: (i, 0))
    return pl.pallas_call(
        _merge_kernel,
        grid=(n // tm,),
        in_specs=[row(d), row(BRANCH_W), row(BRANCH_W), row(BRANCH_W), _full(g1.shape), _full(w_gate.shape),
                  _full(w_branch.shape), _full(w_out.shape), _full(g2.shape), _full(w_route.shape),
                  _full(b_route.shape)],
        out_specs=[row(d), row(d), row(ROUTE_LANES)],
        out_shape=[jax.ShapeDtypeStruct((n, d), F32), jax.ShapeDtypeStruct((n, d), BF16),
                   jax.ShapeDtypeStruct((n, ROUTE_LANES), F32)],
        compiler_params=_params("parallel"),
        name="merge",
    )(x2, oa, ob, oc, g1, w_gate, w_branch, w_out, g2, w_route, b_route)


def _moe_kernel(h_ref, comb_ref, x1_ref, wg_ref, wu_ref, wd_ref, o_ref):
    e = pl.program_id(1)

    @pl.when(e == 0)
    def _():
        o_ref[...] = x1_ref[...]

    comb = comb_ref[...]
    lane = lax.broadcasted_iota(jnp.int32, comb.shape, 1)
    c = jnp.sum(jnp.where(lane == e, comb, 0.0), axis=-1, keepdims=True)
    h = h_ref[...]
    hg = jnp.dot(h, wg_ref[0], preferred_element_type=F32)
    hu = jnp.dot(h, wu_ref[0], preferred_element_type=F32)
    act = hg * _sigmoid(hg) * hu * c
    o_ref[...] += jnp.dot(act.astype(BF16), wd_ref[0], preferred_element_type=F32)


def _moe(h2, comb, x1, w_gate, w_up, w_down, tm):
    n, d = x1.shape
    ne, _, f = w_gate.shape
    row = lambda w: pl.BlockSpec((tm, w), lambda i, e: (i, 0))
    return pl.pallas_call(
        _moe_kernel,
        grid=(n // tm, ne),
        in_specs=[row(d), row(ROUTE_LANES), row(d),
                  pl.BlockSpec((1, d, f), lambda i, e: (e, 0, 0)),
                  pl.BlockSpec((1, d, f), lambda i, e: (e, 0, 0)),
                  pl.BlockSpec((1, f, d), lambda i, e: (e, 0, 0))],
        out_specs=row(d),
        out_shape=jax.ShapeDtypeStruct((n, d), F32),
        compiler_params=_params("parallel", "arbitrary"),
        name="moe",
    )(h2, comb, x1, w_gate, w_up, w_down)


def _seg_ones():
    i = jnp.arange(MXU_HALF) // HEAD
    return (i[:, None] == i[None, :]).astype(BF16)


def _layer_consts(prm):
    row = lambda a: a.reshape(1, -1).astype(F32)
    w_in = prm["w_in"]
    off_g = 2 * BRANCH_W + B_COLS + 3 * BRANCH_W
    zeros = jnp.zeros((LORA, BRANCH_W), F32)
    c = dict(
        g1=row(prm["norm1_g"]), g2=row(prm["norm2_g"]),
        w_abc=w_in[:, :off_g].astype(BF16), w_gate=w_in[:, off_g:].astype(BF16),
        qg=row(jnp.tile(prm["q_norm_g"], N_HEADS)), kg=row(jnp.tile(prm["k_norm_g"], N_HEADS)),
        cw=jnp.pad(prm["conv_w"], ((0, 1), (0, 0))), cb=row(prm["conv_b"]),
        lg=row(prm["conv_ln_g"]), lb=row(prm["conv_ln_b"]),
        wkv=[row(prm["rwkv_mu"]), row(prm["rwkv_w0"]), row(prm["rwkv_a0"]),
             jnp.concatenate([prm["rwkv_w_up"], zeros], axis=0), jnp.concatenate([zeros, prm["rwkv_a_up"]], axis=0),
             prm["rwkv_g_up"], row(prm["rwkv_k_k"]), row(prm["rwkv_k_a"]), row(prm["rwkv_r_k"]),
             row(prm["rwkv_gn_g"]), row(prm["rwkv_gn_b"])],
        w_branch=prm["w_branch"].astype(BF16), w_out=prm["w_out"].astype(BF16),
        w_route=jnp.pad(jnp.concatenate([prm["moe_w_expert"], prm["moe_w_group"]], axis=1),
                        ((0, 0), (0, ROUTE_LANES - N_EXPERTS - N_GROUPS))),
        b_route=jnp.pad(jnp.concatenate([prm["moe_b_expert"], prm["moe_b_group"]]),
                        (0, ROUTE_LANES - N_EXPERTS - N_GROUPS)).reshape(1, ROUTE_LANES),
        moe_g=prm["moe_w_gate"].astype(BF16), moe_u=prm["moe_w_up"].astype(BF16),
        moe_d=prm["moe_w_down"].astype(BF16),
    )
    return c


def _state_in(wkv):
    b = wkv.shape[0]
    return jnp.transpose(wkv, (0, 2, 1, 3)).reshape(b * HEAD, BRANCH_W)


def _state_out(s, b):
    return jnp.transpose(s.reshape(b, HEAD, N_HEADS, HEAD), (0, 2, 1, 3))


def _trunk_layer(x, c, conv_buf, shift_prev, wkv_prev, attend, e256, eye, tiles):
    b, t, d = x.shape
    tm, tt, nb, tc, tm_moe = tiles
    x2 = x.reshape(b * t, d)
    hglu, p, q, k, v = _inproj(x2, c["g1"], c["w_abc"], c["qg"], c["kg"], e256, tm)
    hglu = hglu.reshape(b, t, BRANCH_W)
    p = p.reshape(b, t, B_COLS)

    buf32 = jnp.pad(conv_buf, ((0, 0), (HALO - (CONV_WIDTH - 1), 0), (0, 0)))
    o_a = _conv(hglu, hglu if t >= HALO else buf32, buf32, c["cw"], c["cb"], c["lg"], c["lb"], tt)
    conv_new = jnp.concatenate([conv_buf, hglu], axis=1)[:, -(CONV_WIDTH - 1):]

    o_b, s_new = _wkv(p, shift_prev[:, None, :], _state_in(wkv_prev), c["wkv"], e256, eye, nb, tc)
    shift_new = p[:, -1]
    wkv_new = _state_out(s_new, b)

    q3 = q.reshape(b, t, BRANCH_W)
    k3 = k.reshape(b, t, BRANCH_W)
    v3 = v.reshape(b, t, BRANCH_W)
    o_c = attend(q3, k3, v3)

    x1, h2, comb = _merge(x2, o_a.reshape(b * t, BRANCH_W), o_b.reshape(b * t, BRANCH_W),
                          o_c.reshape(b * t, BRANCH_W), c["g1"], c["w_gate"], c["w_branch"], c["w_out"],
                          c["g2"], c["w_route"], c["b_route"], tm)
    x_out = _moe(h2, comb, x1, c["moe_g"], c["moe_u"], c["moe_d"], tm_moe).reshape(b, t, d)
    return (x_out, k3.reshape(b, t, N_HEADS, HEAD), v3.reshape(b, t, N_HEADS, HEAD), conv_new, shift_new, wkv_new)


def _attend_prompt(q3, k3, v3, slopes_q):
    b, s, _ = q3.shape
    heads = lambda a: jnp.transpose(a.reshape(b, s, N_HEADS, HEAD), (0, 2, 1, 3))
    qh, kh = heads(q3), heads(k3)
    vt = jnp.transpose(v3.reshape(b, s, N_HEADS, HEAD), (0, 2, 3, 1)).astype(BF16)
    ot = _moba_prompt(qh, kh.astype(BF16), vt, _kmean(kh), slopes_q)
    return jnp.transpose(ot, (0, 3, 1, 2)).reshape(b, s, BRANCH_W)


def _attend_sample(q3, k3, v3, cache_k, cache_v, layer, page_table, sc):
    db, t_new, _ = q3.shape
    q4 = q3.reshape(db, t_new, N_HEADS, HEAD) * (HEAD ** -0.5)
    qblk = jnp.einsum("bthd,hg->bhdgt", q4, jnp.eye(N_HEADS, dtype=F32)).reshape(db, BRANCH_W, N_HEADS * t_new)
    qblk = jnp.pad(qblk, ((0, 0), (0, 0), (0, QL - N_HEADS * t_new)))
    qt = jnp.transpose(q4, (0, 2, 1, 3)).reshape(db, N_HEADS * t_new, HEAD)
    kmean, o_part, m, l = _moba_pages(page_table, cache_k, cache_v, layer, qt, sc["table"], sc["rowc"], sc["slc"])
    nblk = kmean.shape[1]
    lanes = lambda a, fill: jnp.pad(jnp.transpose(a[:, :, :nblk], (0, 2, 1)),
                                    ((0, 0), (0, 0), (0, QL - N_HEADS * t_new)), constant_values=fill)
    o_part = jnp.transpose(o_part.reshape(db, nblk, N_HEADS, t_new, HEAD), (0, 3, 1, 2, 4))
    return _moba_combine(kmean.reshape(db, nblk, BRANCH_W), lanes(m, 0.0), lanes(l, 1.0),
                         o_part.reshape(db, t_new, nblk, BRANCH_W), qblk, k3, v3, sc["new_bias"], sc["expand"])


def _sample_consts(t_new, past, slopes):
    lane = jnp.arange(QL)
    used = lane < N_HEADS * t_new
    head = jnp.where(used, lane // t_new, 0)
    step = (lane % t_new).astype(F32)
    slope_lane = jnp.where(used, slopes[head], 0.0).astype(F32)
    new_key = jnp.arange(t_new, dtype=F32)
    new_bias = jnp.where(new_key[:, None] <= step[None, :],
                         slope_lane[None, :] * (new_key[:, None] - step[None, :]), NEG)
    out_head = jnp.arange(BRANCH_W) // HEAD
    expand = jnp.stack([(used & (lane % t_new == t))[:, None] & (head[:, None] == out_head[None, :])
                        for t in range(t_new)]).astype(BF16)
    row_head = jnp.arange(N_HEADS * t_new) // t_new
    row_step = (jnp.arange(N_HEADS * t_new) % t_new).astype(F32)
    row_slope = slopes[row_head].astype(F32)
    col = jnp.arange(MOBA_BLOCK * N_HEADS)
    table = jnp.where(row_head[:, None] == (col % N_HEADS)[None, :],
                      row_slope[:, None] * (col // N_HEADS).astype(F32)[None, :], NEG)
    wide = lambda a: jnp.broadcast_to(a[:, None], (a.shape[0], QL)).astype(F32)
    return dict(new_bias=new_bias.astype(F32), expand=expand, table=table.astype(F32),
                rowc=wide(-row_slope * (past + row_step)), slc=wide(row_slope))


def kernel(x_prompt, x_sample, cache_k, cache_v, page_table, state_conv, state_shift, state_wkv, norm1_g, w_in, conv_w, conv_b, conv_ln_g, conv_ln_b, rwkv_mu, rwkv_w0, rwkv_w_up, rwkv_a0, rwkv_a_up, rwkv_g_up, rwkv_k_k, rwkv_k_a, rwkv_r_k, rwkv_gn_g, rwkv_gn_b, q_norm_g, k_norm_g, w_branch, w_out, norm2_g, moe_w_group, moe_b_group, moe_w_expert, moe_b_expert, moe_w_gate, moe_w_up, moe_w_down):
    stacked = dict(norm1_g=norm1_g, w_in=w_in, conv_w=conv_w, conv_b=conv_b, conv_ln_g=conv_ln_g,
                   conv_ln_b=conv_ln_b, rwkv_mu=rwkv_mu, rwkv_w0=rwkv_w0, rwkv_w_up=rwkv_w_up, rwkv_a0=rwkv_a0,
                   rwkv_a_up=rwkv_a_up, rwkv_g_up=rwkv_g_up, rwkv_k_k=rwkv_k_k, rwkv_k_a=rwkv_k_a,
                   rwkv_r_k=rwkv_r_k, rwkv_gn_g=rwkv_gn_g, rwkv_gn_b=rwkv_gn_b, q_norm_g=q_norm_g,
                   k_norm_g=k_norm_g, w_branch=w_branch, w_out=w_out, norm2_g=norm2_g, moe_w_group=moe_w_group,
                   moe_b_group=moe_b_group, moe_w_expert=moe_w_expert, moe_b_expert=moe_b_expert,
                   moe_w_gate=moe_w_gate, moe_w_up=moe_w_up, moe_w_down=moe_w_down)
    depth = w_in.shape[0]
    b, s, d = x_prompt.shape
    db, t_new, _ = x_sample.shape
    past = page_table.shape[1] * PAGE

    slopes = jnp.exp2(-8.0 * (jnp.arange(N_HEADS, dtype=F32) + 1.0) / N_HEADS)
    slopes_q = jnp.broadcast_to(slopes[:, None, None], (N_HEADS, 1, MOBA_BLOCK))
    e256 = _seg_ones()
    sc = _sample_consts(t_new, past, slopes)
    eye = (jnp.arange(HEAD)[:, None] == (jnp.arange(BRANCH_W) % HEAD)[None, :]).astype(F32)

    tiles_p = (512, 512, b, 256, 1024)
    tiles_s = (db * t_new, t_new, 2, t_new, db * t_new)
    xp, xs = x_prompt, x_sample
    outs = [[] for _ in range(10)]
    for l in range(depth):
        c = _layer_consts({k: v[l] for k, v in stacked.items()})
        xp, kp, vp, cp, sp, wp = _trunk_layer(
            xp, c, jnp.zeros((b, CONV_WIDTH - 1, BRANCH_W), F32), jnp.zeros((b, B_COLS), F32),
            jnp.zeros((b, N_HEADS, HEAD, HEAD), F32), functools.partial(_attend_prompt, slopes_q=slopes_q),
            e256, eye, tiles_p)
        attend_s = functools.partial(_attend_sample, cache_k=cache_k, cache_v=cache_v, layer=l,
                                     page_table=page_table, sc=sc)
        xs, ks, vs, cs, ss, ws = _trunk_layer(xs, c, state_conv[l], state_shift[l], state_wkv[l], attend_s,
                                              e256, eye, tiles_s)
        new = (kp.reshape(b, s // PAGE, PAGE, N_HEADS, HEAD), vp.reshape(b, s // PAGE, PAGE, N_HEADS, HEAD),
               ks, vs, cp, cs, sp, ss, wp, ws)
        for acc, val in zip(outs, new):
            acc.append(val)
    return (xp, xs) + tuple(jnp.stack(a) for a in outs)
```

```python
import functools
import math

import jax
import jax.numpy as jnp
from jax import lax
from jax.experimental import pallas as pl
from jax.experimental.pallas import tpu as pltpu

F32 = jnp.float32
BF16 = jnp.bfloat16
HIGHEST = lax.Precision.HIGHEST

HEAD = 64
N_HEADS = 8
BRANCH_W = N_HEADS * HEAD
CONV_WIDTH = 31
LORA = 64
LORA_G = 128
B_COLS = 3 * BRANCH_W + 2 * LORA + LORA_G
MOBA_BLOCK = 256
MOBA_TOPK = 3
Q_BLOCK = 128
PAGE = 128
N_GROUPS = 4
PER_GROUP = 4
N_EXPERTS = N_GROUPS * PER_GROUP
EPS = 1e-6
GN_EPS = 64e-5
NEG = -1e30
MXU_HALF = 256
VMEM_LIMIT = 56 * 1024 * 1024


def _sigmoid(x):
    return 1.0 / (1.0 + jnp.exp(-x))


def _split_dot(x, e, passes):
    acc = None
    rem = x
    for i in range(passes):
        hi = rem.astype(BF16)
        d = jnp.dot(hi, e, preferred_element_type=F32)
        acc = d if acc is None else acc + d
        if i + 1 < passes:
            rem = rem - hi.astype(F32)
    return acc


def _seg_sum(x, e256, passes):
    parts = [_split_dot(x[:, i:i + MXU_HALF], e256, passes) for i in range(0, x.shape[1], MXU_HALF)]
    return jnp.concatenate(parts, axis=1)


def _params(*sem):
    return pltpu.CompilerParams(dimension_semantics=sem, vmem_limit_bytes=VMEM_LIMIT)


def _full(shape):
    n = len(shape)
    return pl.BlockSpec(shape, lambda *_: (0,) * n)


def _inproj_kernel(x_ref, g_ref, w_ref, qg_ref, kg_ref, e_ref, hglu_ref, p_ref, q_ref, k_ref, v_ref):
    x = x_ref[...]
    h = (x * lax.rsqrt(jnp.mean(x * x, axis=-1, keepdims=True) + EPS) * g_ref[...]).astype(BF16)
    c0, c1, c2 = 2 * BRANCH_W, 2 * BRANCH_W + B_COLS, 2 * BRANCH_W + B_COLS + 3 * BRANCH_W
    u = jnp.dot(h, w_ref[:, :c0], preferred_element_type=F32)
    hglu_ref[...] = u[:, :BRANCH_W] * _sigmoid(u[:, BRANCH_W:])
    p_ref[...] = jnp.dot(h, w_ref[:, c0:c1], preferred_element_type=F32)
    qkv = jnp.dot(h, w_ref[:, c1:c2], preferred_element_type=F32)
    e = e_ref[...]
    q = qkv[:, :BRANCH_W]
    k = qkv[:, BRANCH_W:2 * BRANCH_W]
    q_ref[...] = q * lax.rsqrt(_seg_sum(q * q, e, 2) * (1.0 / HEAD) + EPS) * qg_ref[...]
    k_ref[...] = k * lax.rsqrt(_seg_sum(k * k, e, 2) * (1.0 / HEAD) + EPS) * kg_ref[...]
    v_ref[...] = qkv[:, 2 * BRANCH_W:]


def _inproj(x2, g1, w_abc, qg, kg, e256, tm):
    n, d = x2.shape
    row = lambda w: pl.BlockSpec((tm, w), lambda i: (i, 0))
    outs = [BRANCH_W, B_COLS, BRANCH_W, BRANCH_W, BRANCH_W]
    return pl.pallas_call(
        _inproj_kernel,
        grid=(n // tm,),
        in_specs=[row(d), _full(g1.shape), _full(w_abc.shape), _full(qg.shape), _full(kg.shape), _full(e256.shape)],
        out_specs=[row(w) for w in outs],
        out_shape=[jax.ShapeDtypeStruct((n, w), F32) for w in outs],
        compiler_params=_params("parallel"),
        name="inproj",
    )(x2, g1, w_abc, qg, kg, e256)


HALO = 32


def _conv_kernel(h_ref, halo_ref, buf_ref, cw_ref, cb_ref, lg_ref, lb_ref, o_ref, ext):
    i = pl.program_id(1)
    tt = h_ref.shape[1]

    @pl.when(i == 0)
    def _():
        ext[0:HALO, :] = buf_ref[0]

    @pl.when(i > 0)
    def _():
        ext[0:HALO, :] = halo_ref[0]

    ext[HALO:, :] = h_ref[0]
    first = HALO - (CONV_WIDTH - 1)
    acc = ext[first:first + tt, :] * cw_ref[0:1, :]
    for w in range(1, CONV_WIDTH):
        acc = acc + ext[first + w:first + w + tt, :] * cw_ref[w:w + 1, :]
    y = acc + cb_ref[...]
    mu = jnp.mean(y, axis=-1, keepdims=True)
    d = y - mu
    var = jnp.mean(d * d, axis=-1, keepdims=True)
    y = d * lax.rsqrt(var + 1e-5) * lg_ref[...] + lb_ref[...]
    o_ref[0] = y * _sigmoid(y)


def _conv(hglu, halo_src, buf32, cw, cb, lg, lb, tt):
    b, t, c = hglu.shape
    per = max(tt // HALO, 1)
    return pl.pallas_call(
        _conv_kernel,
        grid=(b, t // tt),
        in_specs=[
            pl.BlockSpec((1, tt, c), lambda bi, i: (bi, i, 0)),
            pl.BlockSpec((1, HALO, c), lambda bi, i: (bi, jnp.maximum(i * per - 1, 0), 0)),
            pl.BlockSpec((1, HALO, c), lambda bi, i: (bi, 0, 0)),
            _full(cw.shape), _full(cb.shape), _full(lg.shape), _full(lb.shape),
        ],
        out_specs=pl.BlockSpec((1, tt, c), lambda bi, i: (bi, i, 0)),
        out_shape=jax.ShapeDtypeStruct((b, t, c), F32),
        scratch_shapes=[pltpu.VMEM((HALO + tt, c), F32)],
        compiler_params=_params("parallel", "arbitrary"),
        name="conv",
    )(hglu, halo_src, buf32, cw, cb, lg, lb)


STEP_GROUP = 8


def _wkv_kernel(p_ref, shift_ref, s0_ref, mu_ref, w0_ref, a0_ref, wup_ref, aup_ref, gup_ref,
                kkw_ref, ka_ref, rk_ref, gng_ref, gnb_ref, e_ref, eye_ref,
                o_ref, sout_ref,
                state, pext, r_s, w_s, k_s, kk_s, b_s, v_s, g_s, o_s):
    c = pl.program_id(1)
    nb, tc = p_ref.shape[0], p_ref.shape[1]
    e = e_ref[...]
    w_off = 3 * BRANCH_W

    @pl.when(c == 0)
    def _():
        state[...] = s0_ref[...]

    for b in range(nb):
        @pl.when(c == 0)
        def _():
            pext[b, 7:8, :] = shift_ref[b]

        @pl.when(c > 0)
        def _():
            pext[b, 7:8, :] = pext[b, tc + 7:tc + 8, :]

        p = p_ref[b]
        pext[b, 8:tc + 8, :] = p
        xm = p + (pext[b, 7:tc + 7, :] - p) * mu_ref[...]
        r = xm[:, :BRANCH_W]
        k = xm[:, BRANCH_W:2 * BRANCH_W]
        v = xm[:, 2 * BRANCH_W:w_off]
        wa = xm[:, w_off:w_off + 2 * LORA]
        gd = xm[:, w_off + 2 * LORA:]
        w_raw = w0_ref[...] + jnp.dot(jnp.tanh(wa), wup_ref[...], precision=HIGHEST, preferred_element_type=F32)
        a = _sigmoid(a0_ref[...] + jnp.dot(wa, aup_ref[...], precision=HIGHEST, preferred_element_type=F32))
        g = jnp.dot(_sigmoid(gd), gup_ref[...], precision=HIGHEST, preferred_element_type=F32)
        kk = k * kkw_ref[...]
        kk = kk / jnp.maximum(jnp.sqrt(_seg_sum(kk * kk, e, 3)), 1e-12)
        r_s[b] = r
        w_s[b] = jnp.exp(-math.exp(-0.5) * _sigmoid(w_raw))
        k_s[b] = k * (1.0 + (a - 1.0) * ka_ref[...])
        kk_s[b] = kk
        b_s[b] = kk * a
        v_s[b] = v
        g_s[b] = g

    eye = eye_ref[...]
    e2 = jnp.concatenate([e, e], axis=0)

    def bc(block, j):
        return jnp.broadcast_to(block[j:j + 1, :], (HEAD, BRANCH_W))

    def halves(hi, lo):
        return [jnp.concatenate([hi[:, i:i + MXU_HALF], lo[:, i:i + MXU_HALF]], axis=1)
                for i in range(0, BRANCH_W, MXU_HALF)]

    def unhalve(res, j, n):
        return jnp.concatenate([res[j * HEAD:(j + 1) * HEAD], res[(n + j) * HEAD:(n + j + 1) * HEAD]], axis=1)

    def group(t8, carry):
        base = pl.multiple_of(t8 * STEP_GROUP, STEP_GROUP)
        bs = range(nb)
        kk8, v8, w8, b8, k8, r8 = ([ref[b, pl.ds(base, STEP_GROUP), :] for b in bs]
                                   for ref in (kk_s, v_s, w_s, b_s, k_s, r_s))
        vk = []
        for b in bs:
            v_hi = v8[b].astype(BF16).astype(F32)
            v_lo = v8[b] - v_hi
            hi = jnp.concatenate([(bc(v_hi, j) * eye).astype(BF16) for j in range(STEP_GROUP)], axis=0)
            lo = jnp.concatenate([(bc(v_lo, j) * eye).astype(BF16) for j in range(STEP_GROUP)], axis=0)
            res = jnp.dot(jnp.concatenate(halves(hi, lo), axis=0), e2, preferred_element_type=F32)
            vk.append([unhalve(res, j, STEP_GROUP) * bc(k8[b], j) for j in range(STEP_GROUP)])
        s = [state[b * HEAD:(b + 1) * HEAD, :] for b in bs]
        outs = [[] for _ in bs]
        stack = lambda a: [a[:, :MXU_HALF], a[:, MXU_HALF:]]
        out_row = lambda ro: jnp.sum(ro * eye, axis=0, keepdims=True)
        ys = [None for _ in bs]
        for j in range(STEP_GROUP):
            lhs = []
            for b in bs:
                x = s[b] * bc(kk8[b], j)
                x_hi = x.astype(BF16)
                parts = stack(x_hi) + stack((x - x_hi.astype(F32)).astype(BF16))
                lhs.append(jnp.concatenate(parts + (stack(ys[b]) if j else []), axis=0))
            res = [jnp.dot(lhs[b], e, preferred_element_type=F32) for b in bs]
            for b in bs:
                sk = unhalve(res[b], 0, 1) + unhalve(res[b], 2, 1)
                if j:
                    outs[b].append(out_row(unhalve(res[b], 4, 1)))
                s[b] = s[b] * bc(w8[b], j) - sk * bc(b8[b], j) + vk[b][j]
                ys[b] = (s[b] * bc(r8[b], j)).astype(BF16)
        last = jnp.dot(jnp.concatenate([h for b in bs for h in stack(ys[b])], axis=0), e, preferred_element_type=F32)
        for b in bs:
            outs[b].append(out_row(unhalve(last, 2 * b, 1)))
            state[b * HEAD:(b + 1) * HEAD, :] = s[b]
            o_s[b, pl.ds(base, STEP_GROUP), :] = jnp.concatenate(outs[b], axis=0)
        return carry

    lax.fori_loop(0, tc // STEP_GROUP, group, 0)

    for b in range(nb):
        o = o_s[b]
        d = o - _seg_sum(o, e, 3) * (1.0 / HEAD)
        var = _seg_sum(d * d, e, 3) * (1.0 / HEAD)
        y = d * lax.rsqrt(var + GN_EPS) * gng_ref[...] + gnb_ref[...]
        bonus = _seg_sum(r_s[b] * k_s[b] * rk_ref[...], e, 3) * v_s[b]
        o_ref[b] = (y + bonus) * g_s[b]

    @pl.when(c == pl.num_programs(1) - 1)
    def _():
        sout_ref[...] = state[...]


def _wkv(p, shift, s0, consts, e256, eye, nb, tc):
    b, t, _ = p.shape
    chunk = lambda w: pltpu.VMEM((nb, tc, w), F32)
    return pl.pallas_call(
        _wkv_kernel,
        grid=(b // nb, t // tc),
        in_specs=[
            pl.BlockSpec((nb, tc, B_COLS), lambda bi, c: (bi, c, 0)),
            pl.BlockSpec((nb, 1, B_COLS), lambda bi, c: (bi, 0, 0)),
            pl.BlockSpec((nb * HEAD, BRANCH_W), lambda bi, c: (bi, 0)),
        ] + [_full(a.shape) for a in consts] + [_full(e256.shape), _full(eye.shape)],
        out_specs=[
            pl.BlockSpec((nb, tc, BRANCH_W), lambda bi, c: (bi, c, 0)),
            pl.BlockSpec((nb * HEAD, BRANCH_W), lambda bi, c: (bi, 0)),
        ],
        out_shape=[jax.ShapeDtypeStruct((b, t, BRANCH_W), F32), jax.ShapeDtypeStruct((b * HEAD, BRANCH_W), F32)],
        scratch_shapes=[pltpu.VMEM((nb * HEAD, BRANCH_W), F32), pltpu.VMEM((nb, tc + 8, B_COLS), F32)]
        + [chunk(BRANCH_W) for _ in range(8)],
        compiler_params=_params("parallel", "arbitrary"),
        name="wkv",
    )(p, shift, s0, *consts, e256, eye)


def _top3(gate, valid, idx):
    sel = jnp.zeros(gate.shape, jnp.bool_)
    avail = valid
    big = float(gate.shape[0])
    for _ in range(MOBA_TOPK):
        g = jnp.where(avail, gate, -jnp.inf)
        top = jnp.max(g, axis=0, keepdims=True)
        first = jnp.min(jnp.where(avail & (g == top), idx, big), axis=0, keepdims=True)
        pick = idx == first
        sel = sel | pick
        avail = avail & jnp.logical_not(pick)
    return sel


def _kmean_kernel(k_ref, o_ref):
    k = k_ref[0, 0]
    nb = k.shape[0] // MOBA_BLOCK
    o_ref[0, 0] = jnp.mean(k.reshape(nb, MOBA_BLOCK, HEAD), axis=1)


def _kmean(k_heads):
    b, h, s, d = k_heads.shape
    nb = s // MOBA_BLOCK
    return pl.pallas_call(
        _kmean_kernel,
        grid=(b, h),
        in_specs=[pl.BlockSpec((1, 1, s, d), lambda bi, hi: (bi, hi, 0, 0))],
        out_specs=pl.BlockSpec((1, 1, nb, d), lambda bi, hi: (bi, hi, 0, 0)),
        out_shape=jax.ShapeDtypeStruct((b, h, nb, d), F32),
        compiler_params=_params("parallel", "parallel"),
        name="kmean",
    )(k_heads)


HEAD_GROUP = 4
LOG2E = 1.4426950408889634


def _moba_prompt_kernel(q_ref, k_ref, vt_ref, km_ref, slope_ref, o_ref, selb):
    j_own = pl.program_id(2)
    hg = q_ref.shape[1]
    nt = (((1,), (1,)), ((), ()))
    key = lax.broadcasted_iota(jnp.int32, (MOBA_BLOCK, MOBA_BLOCK), 0)
    qry = lax.broadcasted_iota(jnp.int32, (MOBA_BLOCK, MOBA_BLOCK), 1)
    rel = (key - qry).astype(F32)
    causal = key <= qry

    hs = range(hg)
    slope2 = [slope_ref[h] * LOG2E for h in hs]
    qf = [q_ref[0, h] * (HEAD ** -0.5) for h in hs]
    qs = [(qf[h] * LOG2E).astype(BF16) for h in hs]
    col_bias = [slope2[h] * key[:, 0:1].astype(F32) for h in hs]

    def qk(h, start):
        return lax.dot_general(k_ref[0, h, pl.ds(start, MOBA_BLOCK), :], qs[h], nt, preferred_element_type=F32)

    def pv(h, start, p):
        return jnp.dot(vt_ref[0, h, :, pl.ds(start, MOBA_BLOCK)], p.astype(BF16), preferred_element_type=F32)

    own = pl.multiple_of(j_own * MOBA_BLOCK, MOBA_BLOCK)
    s_own = [qk(h, own) for h in hs]
    for h in hs:
        gate = lax.dot_general(km_ref[0, h], qf[h], nt, precision=HIGHEST, preferred_element_type=F32)
        blk = lax.broadcasted_iota(jnp.int32, gate.shape, 0)
        sel = _top3(gate, blk < j_own, blk.astype(F32))
        dist = ((blk - j_own) * MOBA_BLOCK - qry[0:1, :]).astype(F32)
        selb[h] = jnp.where(sel, slope2[h] * dist, NEG)
    ms, ls, ps = [], [], []
    for h in hs:
        s = jnp.where(causal, s_own[h] + slope2[h] * rel, NEG)
        m = jnp.max(s, axis=0, keepdims=True)
        p = jnp.exp2(s - m)
        ms.append(m)
        ls.append(jnp.sum(p, axis=0, keepdims=True))
        ps.append(p)
    accs = [pv(h, own, ps[h]) for h in hs]

    def body(n, carry):
        ms, ls, accs = carry
        start = pl.multiple_of(n * MOBA_BLOCK, MOBA_BLOCK)
        ss = [qk(h, start) for h in hs]
        m_out, l_out, alphas, ps = [], [], [], []
        for h in hs:
            s = ss[h] + col_bias[h] + selb[h, pl.ds(n, 1), :]
            m_new = jnp.maximum(ms[h], jnp.max(s, axis=0, keepdims=True))
            alpha = jnp.exp2(ms[h] - m_new)
            p = jnp.exp2(s - m_new)
            m_out.append(m_new)
            l_out.append(alpha * ls[h] + jnp.sum(p, axis=0, keepdims=True))
            alphas.append(alpha)
            ps.append(p)
        acc_out = [alphas[h] * accs[h] + pv(h, start, ps[h]) for h in hs]
        return tuple(m_out), tuple(l_out), tuple(acc_out)

    ms, ls, accs = lax.fori_loop(0, j_own, body, (tuple(ms), tuple(ls), tuple(accs)))
    for h in hs:
        o_ref[0, h] = accs[h] / ls[h]


def _moba_prompt(q_heads, k_heads_bf, vt_heads_bf, kmean, slopes):
    b, h, s, d = q_heads.shape
    nb = s // MOBA_BLOCK
    hg = HEAD_GROUP
    return pl.pallas_call(
        _moba_prompt_kernel,
        grid=(b, h // hg, nb),
        in_specs=[
            pl.BlockSpec((1, hg, MOBA_BLOCK, d), lambda bi, hi, i: (bi, hi, i, 0)),
            pl.BlockSpec((1, hg, s, d), lambda bi, hi, i: (bi, hi, 0, 0)),
            pl.BlockSpec((1, hg, d, s), lambda bi, hi, i: (bi, hi, 0, 0)),
            pl.BlockSpec((1, hg, nb, d), lambda bi, hi, i: (bi, hi, 0, 0)),
            pl.BlockSpec((hg, 1, MOBA_BLOCK), lambda bi, hi, i: (hi, 0, 0)),
        ],
        out_specs=pl.BlockSpec((1, hg, d, MOBA_BLOCK), lambda bi, hi, i: (bi, hi, 0, i)),
        out_shape=jax.ShapeDtypeStruct((b, h, d, s), F32),
        scratch_shapes=[pltpu.VMEM((hg, nb, MOBA_BLOCK), F32)],
        compiler_params=_params("parallel", "parallel", "arbitrary"),
        name="moba_prompt",
    )(q_heads, k_heads_bf, vt_heads_bf, kmean, slopes)


QL = 128


BLOCKS_PER_STEP = 2
PAGES_PER_BLOCK = MOBA_BLOCK // PAGE


def _moba_pages_kernel(pt_ref, *refs):
    npg = BLOCKS_PER_STEP * PAGES_PER_BLOCK
    k_refs, v_refs = refs[:npg], refs[npg:2 * npg]
    qd_ref, colb_ref, rowc_ref, slc_ref, hm_ref, km_ref, o_ref, m_ref, l_ref = refs[2 * npg:]
    n = pl.program_id(1)
    t_new = o_ref.shape[2]

    @pl.when(n == 0)
    def _():
        km_ref[...] = jnp.zeros(km_ref.shape, F32)
        m_ref[...] = jnp.zeros(m_ref.shape, F32)
        l_ref[...] = jnp.ones(l_ref.shape, F32)

    q = qd_ref[0].astype(BF16)
    lane = lax.broadcasted_iota(jnp.int32, (1, QL), 1)
    nt = (((1,), (1,)), ((), ()))
    us = range(BLOCKS_PER_STEP)
    pages = [range(u * PAGES_PER_BLOCK, (u + 1) * PAGES_PER_BLOCK) for u in us]
    block = lambda refs, u: jnp.concatenate([refs[i][0, 0].reshape(BRANCH_W, PAGE) for i in pages[u]], axis=1)
    kts = [block(k_refs, u) for u in us]
    ss = [jnp.dot(q, kts[u].astype(BF16), preferred_element_type=F32) for u in us]
    ps = []
    for u in us:
        blk = n * BLOCKS_PER_STEP + u
        here = lane == blk
        km_ref[0] = jnp.where(here, jnp.sum(kts[u], axis=1, keepdims=True) * (1.0 / MOBA_BLOCK), km_ref[0])
        s = ss[u] + colb_ref[...] + (rowc_ref[:, 0:1] + slc_ref[:, 0:1] * (blk * MOBA_BLOCK).astype(F32))
        m = jnp.max(s, axis=1, keepdims=True)
        p = jnp.exp(s - m)
        m_ref[0] = jnp.where(here, m, m_ref[0])
        l_ref[0] = jnp.where(here, jnp.sum(p, axis=1, keepdims=True), l_ref[0])
        ps.append(p.astype(BF16))
    for u in us:
        full = lax.dot_general(ps[u], block(v_refs, u).astype(BF16), nt, preferred_element_type=F32)
        acc = full[0:t_new, :] * hm_ref[0:1, :]
        for h in range(1, N_HEADS):
            acc = acc + full[h * t_new:(h + 1) * t_new, :] * hm_ref[h:h + 1, :]
        o_ref[0, u] = acc


def _moba_pages(page_table, cache_kt, cache_vt, layer, qdiag, colb, rowc, slc, head_mask, t_new):
    db, n_pages = page_table.shape
    nblk = n_pages // PAGES_PER_BLOCK
    npg = BLOCKS_PER_STEP * PAGES_PER_BLOCK
    rows = qdiag.shape[1]
    page = lambda off: pl.BlockSpec((1, 1, N_HEADS, HEAD, PAGE),
                                    lambda b, n, pt: (layer, pt[b, npg * n + off], 0, 0, 0))
    const = lambda a: pl.BlockSpec(a.shape, lambda b, n, pt: (0, 0))
    stat = lambda r: pl.BlockSpec((1, r, QL), lambda b, n, pt: (b, 0, 0))
    return pl.pallas_call(
        _moba_pages_kernel,
        grid_spec=pltpu.PrefetchScalarGridSpec(
            num_scalar_prefetch=1,
            grid=(db, nblk // BLOCKS_PER_STEP),
            in_specs=[page(i) for i in range(npg)] * 2
            + [pl.BlockSpec((1, rows, BRANCH_W), lambda b, n, pt: (b, 0, 0)),
               const(colb), const(rowc), const(slc), const(head_mask)],
            out_specs=[stat(BRANCH_W),
                       pl.BlockSpec((1, BLOCKS_PER_STEP, t_new, BRANCH_W), lambda b, n, pt: (b, n, 0, 0)),
                       stat(rows), stat(rows)],
        ),
        out_shape=[jax.ShapeDtypeStruct((db, BRANCH_W, QL), F32),
                   jax.ShapeDtypeStruct((db, nblk, t_new, BRANCH_W), F32),
                   jax.ShapeDtypeStruct((db, rows, QL), F32),
                   jax.ShapeDtypeStruct((db, rows, QL), F32)],
        compiler_params=_params("parallel", "arbitrary"),
        name="moba_pages",
    )(page_table, *([cache_kt] * npg), *([cache_vt] * npg), qdiag, colb, rowc, slc, head_mask)


def _moba_combine_kernel(km_ref, m_ref, l_ref, op_ref, qb_ref, kn_ref, vn_ref, nb_ref, ex_ref, o_ref):
    t_new = kn_ref.shape[1]
    qb = qb_ref[0]
    gate = jnp.dot(km_ref[0], qb, precision=HIGHEST, preferred_element_type=F32)
    blk = lax.broadcasted_iota(jnp.int32, gate.shape, 0)
    sel = _top3(gate, blk >= 0, blk.astype(F32))
    m_blk = jnp.where(sel, m_ref[0], NEG)
    s_new = jnp.dot(kn_ref[0].astype(BF16), qb.astype(BF16), preferred_element_type=F32) + nb_ref[...]
    m_all = jnp.maximum(jnp.max(m_blk, axis=0, keepdims=True), jnp.max(s_new, axis=0, keepdims=True))
    w_blk = jnp.where(sel, jnp.exp(m_blk - m_all), 0.0)
    p_new = jnp.exp(s_new - m_all)
    denom = jnp.sum(w_blk * l_ref[0], axis=0, keepdims=True) + jnp.sum(p_new, axis=0, keepdims=True)
    vn = vn_ref[0]
    outs = []
    for t in range(t_new):
        ex = ex_ref[t]
        wt = _split_dot(w_blk, ex, 3)
        pt = _split_dot(p_new, ex, 3)
        num = (jnp.sum(wt * op_ref[0, t], axis=0, keepdims=True)
               + jnp.sum(pt * vn, axis=0, keepdims=True))
        outs.append(num / _split_dot(denom, ex, 3))
    o_ref[0] = jnp.concatenate(outs, axis=0)


def _moba_combine(kmean, m, l, o_part, qblk, k_new, v_new, new_bias, expand):
    db, t_new, nblk, _ = o_part.shape
    lead = lambda shape: pl.BlockSpec((1,) + shape, lambda b: (b,) + (0,) * len(shape))
    return pl.pallas_call(
        _moba_combine_kernel,
        grid=(db,),
        in_specs=[lead((nblk, BRANCH_W)), lead((nblk, QL)), lead((nblk, QL)), lead((t_new, nblk, BRANCH_W)),
                  lead((BRANCH_W, QL)), lead((t_new, BRANCH_W)), lead((t_new, BRANCH_W)),
                  _full(new_bias.shape), _full(expand.shape)],
        out_specs=lead((t_new, BRANCH_W)),
        out_shape=jax.ShapeDtypeStruct((db, t_new, BRANCH_W), F32),
        compiler_params=_params("parallel"),
        name="moba_combine",
    )(kmean, m, l, o_part, qblk, k_new, v_new, new_bias, expand)


ROUTE_LANES = 128


def _merge_kernel(x_ref, oa_ref, ob_ref, oc_ref, g1_ref, wg_ref, wb_ref, wo_ref, g2_ref, wr_ref, br_ref,
                  x1_ref, h2_ref, comb_ref):
    x = x_ref[...]
    d = x.shape[1]
    h = (x * lax.rsqrt(jnp.mean(x * x, axis=-1, keepdims=True) + EPS) * g1_ref[...]).astype(BF16)
    merged = None
    for n, br_ref_n in enumerate((oa_ref, ob_ref, oc_ref)):
        gate = _sigmoid(jnp.dot(h, wg_ref[:, n * d:(n + 1) * d], preferred_element_type=F32))
        br = jnp.dot(br_ref_n[...].astype(BF16), wb_ref[n], preferred_element_type=F32)
        merged = gate * br if merged is None else merged + gate * br
    x1 = x + jnp.dot(merged.astype(BF16), wo_ref[...], preferred_element_type=F32)
    x1_ref[...] = x1
    h2 = x1 * lax.rsqrt(jnp.mean(x1 * x1, axis=-1, keepdims=True) + EPS) * g2_ref[...]
    h2_ref[...] = h2.astype(BF16)

    logits = jnp.dot(h2, wr_ref[...], precision=HIGHEST, preferred_element_type=F32) + br_ref[...]
    lane = lax.broadcasted_iota(jnp.int32, logits.shape, 1)
    lane_f = lane.astype(F32)
    is_g = (lane >= N_EXPERTS) & (lane < N_EXPERTS + N_GROUPS)
    glog = jnp.where(is_g, logits, -jnp.inf)
    gmax = jnp.max(glog, axis=-1, keepdims=True)
    gidx = jnp.min(jnp.where(glog == gmax, lane_f - N_EXPERTS, float(N_GROUPS)), axis=-1, keepdims=True)
    gw = 1.0 / jnp.sum(jnp.where(is_g, jnp.exp(logits - gmax), 0.0), axis=-1, keepdims=True)
    in_grp = (lane < N_EXPERTS) & ((lane // PER_GROUP).astype(F32) == gidx)
    big = float(ROUTE_LANES)
    m1 = jnp.where(in_grp, logits, -jnp.inf)
    t1 = jnp.max(m1, axis=-1, keepdims=True)
    i1 = jnp.min(jnp.where(in_grp & (m1 == t1), lane_f, big), axis=-1, keepdims=True)
    rest = in_grp & (lane_f != i1)
    m2 = jnp.where(rest, logits, -jnp.inf)
    t2 = jnp.max(m2, axis=-1, keepdims=True)
    i2 = jnp.min(jnp.where(rest & (m2 == t2), lane_f, big), axis=-1, keepdims=True)
    e2 = jnp.exp(t2 - t1)
    w1 = 1.0 / (1.0 + e2)
    comb_ref[...] = (jnp.where(lane_f == i1, w1, 0.0) + jnp.where(lane_f == i2, e2 * w1, 0.0)) * gw


def _merge(x2, oa, ob, oc, g1, w_gate, w_branch, w_out, g2, w_route, b_route, tm):
    n, d = x2.shape
    row = lambda w: pl.BlockSpec((tm, w), lambda i: (i, 0))
    return pl.pallas_call(
        _merge_kernel,
        grid=(n // tm,),
        in_specs=[row(d), row(BRANCH_W), row(BRANCH_W), row(BRANCH_W), _full(g1.shape), _full(w_gate.shape),
                  _full(w_branch.shape), _full(w_out.shape), _full(g2.shape), _full(w_route.shape),
                  _full(b_route.shape)],
        out_specs=[row(d), row(d), row(ROUTE_LANES)],
        out_shape=[jax.ShapeDtypeStruct((n, d), F32), jax.ShapeDtypeStruct((n, d), BF16),
                   jax.ShapeDtypeStruct((n, ROUTE_LANES), F32)],
        compiler_params=_params("parallel"),
        name="merge",
    )(x2, oa, ob, oc, g1, w_gate, w_branch, w_out, g2, w_route, b_route)


def _moe_kernel(h_ref, comb_ref, x1_ref, wg_ref, wu_ref, wd_ref, o_ref):
    e = pl.program_id(1)

    @pl.when(e == 0)
    def _():
        o_ref[...] = x1_ref[...]

    comb = comb_ref[...]
    lane = lax.broadcasted_iota(jnp.int32, comb.shape, 1)
    c = jnp.sum(jnp.where(lane == e, comb, 0.0), axis=-1, keepdims=True)
    h = h_ref[...]
    hg = jnp.dot(h, wg_ref[0], preferred_element_type=F32)
    hu = jnp.dot(h, wu_ref[0], preferred_element_type=F32)
    act = hg * _sigmoid(hg) * hu * c
    o_ref[...] += jnp.dot(act.astype(BF16), wd_ref[0], preferred_element_type=F32)


def _moe(h2, comb, x1, w_gate, w_up, w_down, tm):
    n, d = x1.shape
    ne, _, f = w_gate.shape
    row = lambda w: pl.BlockSpec((tm, w), lambda i, e: (i, 0))
    return pl.pallas_call(
        _moe_kernel,
        grid=(n // tm, ne),
        in_specs=[row(d), row(ROUTE_LANES), row(d),
                  pl.BlockSpec((1, d, f), lambda i, e: (e, 0, 0)),
                  pl.BlockSpec((1, d, f), lambda i, e: (e, 0, 0)),
                  pl.BlockSpec((1, f, d), lambda i, e: (e, 0, 0))],
        out_specs=row(d),
        out_shape=jax.ShapeDtypeStruct((n, d), F32),
        compiler_params=_params("parallel", "arbitrary"),
        name="moe",
    )(h2, comb, x1, w_gate, w_up, w_down)


def _seg_ones():
    i = jnp.arange(MXU_HALF) // HEAD
    return (i[:, None] == i[None, :]).astype(BF16)


def _layer_consts(prm):
    row = lambda a: a.reshape(1, -1).astype(F32)
    w_in = prm["w_in"]
    off_g = 2 * BRANCH_W + B_COLS + 3 * BRANCH_W
    zeros = jnp.zeros((LORA, BRANCH_W), F32)
    c = dict(
        g1=row(prm["norm1_g"]), g2=row(prm["norm2_g"]),
        w_abc=w_in[:, :off_g].astype(BF16), w_gate=w_in[:, off_g:].astype(BF16),
        qg=row(jnp.tile(prm["q_norm_g"], N_HEADS)), kg=row(jnp.tile(prm["k_norm_g"], N_HEADS)),
        cw=jnp.pad(prm["conv_w"], ((0, 1), (0, 0))), cb=row(prm["conv_b"]),
        lg=row(prm["conv_ln_g"]), lb=row(prm["conv_ln_b"]),
        wkv=[row(prm["rwkv_mu"]), row(prm["rwkv_w0"]), row(prm["rwkv_a0"]),
             jnp.concatenate([prm["rwkv_w_up"], zeros], axis=0), jnp.concatenate([zeros, prm["rwkv_a_up"]], axis=0),
             prm["rwkv_g_up"], row(prm["rwkv_k_k"]), row(prm["rwkv_k_a"]), row(prm["rwkv_r_k"]),
             row(prm["rwkv_gn_g"]), row(prm["rwkv_gn_b"])],
        w_branch=prm["w_branch"].astype(BF16), w_out=prm["w_out"].astype(BF16),
        w_route=jnp.pad(jnp.concatenate([prm["moe_w_expert"], prm["moe_w_group"]], axis=1),
                        ((0, 0), (0, ROUTE_LANES - N_EXPERTS - N_GROUPS))),
        b_route=jnp.pad(jnp.concatenate([prm["moe_b_expert"], prm["moe_b_group"]]),
                        (0, ROUTE_LANES - N_EXPERTS - N_GROUPS)).reshape(1, ROUTE_LANES),
        moe_g=prm["moe_w_gate"].astype(BF16), moe_u=prm["moe_w_up"].astype(BF16),
        moe_d=prm["moe_w_down"].astype(BF16),
    )
    return c


def _state_in(wkv):
    b = wkv.shape[0]
    return jnp.transpose(wkv, (0, 2, 1, 3)).reshape(b * HEAD, BRANCH_W)


def _state_out(s, b):
    return jnp.transpose(s.reshape(b, HEAD, N_HEADS, HEAD), (0, 2, 1, 3))


def _trunk_layer(x, c, conv_buf, shift_prev, wkv_prev, attend, e256, eye, tiles):
    b, t, d = x.shape
    tm, tt, nb, tc, tm_moe = tiles
    x2 = x.reshape(b * t, d)
    hglu, p, q, k, v = _inproj(x2, c["g1"], c["w_abc"], c["qg"], c["kg"], e256, tm)
    hglu = hglu.reshape(b, t, BRANCH_W)
    p = p.reshape(b, t, B_COLS)

    buf32 = jnp.pad(conv_buf, ((0, 0), (HALO - (CONV_WIDTH - 1), 0), (0, 0)))
    o_a = _conv(hglu, hglu if t >= HALO else buf32, buf32, c["cw"], c["cb"], c["lg"], c["lb"], tt)
    conv_new = jnp.concatenate([conv_buf, hglu], axis=1)[:, -(CONV_WIDTH - 1):]

    o_b, s_new = _wkv(p, shift_prev[:, None, :], _state_in(wkv_prev), c["wkv"], e256, eye, nb, tc)
    shift_new = p[:, -1]
    wkv_new = _state_out(s_new, b)

    q3 = q.reshape(b, t, BRANCH_W)
    k3 = k.reshape(b, t, BRANCH_W)
    v3 = v.reshape(b, t, BRANCH_W)
    o_c = attend(q3, k3, v3)

    x1, h2, comb = _merge(x2, o_a.reshape(b * t, BRANCH_W), o_b.reshape(b * t, BRANCH_W),
                          o_c.reshape(b * t, BRANCH_W), c["g1"], c["w_gate"], c["w_branch"], c["w_out"],
                          c["g2"], c["w_route"], c["b_route"], tm)
    x_out = _moe(h2, comb, x1, c["moe_g"], c["moe_u"], c["moe_d"], tm_moe).reshape(b, t, d)
    return (x_out, k3.reshape(b, t, N_HEADS, HEAD), v3.reshape(b, t, N_HEADS, HEAD), conv_new, shift_new, wkv_new)


def _attend_prompt(q3, k3, v3, slopes_q):
    b, s, _ = q3.shape
    heads = lambda a: jnp.transpose(a.reshape(b, s, N_HEADS, HEAD), (0, 2, 1, 3))
    qh, kh = heads(q3), heads(k3)
    vt = jnp.transpose(v3.reshape(b, s, N_HEADS, HEAD), (0, 2, 3, 1)).astype(BF16)
    ot = _moba_prompt(qh, kh.astype(BF16), vt, _kmean(kh), slopes_q)
    return jnp.transpose(ot, (0, 3, 1, 2)).reshape(b, s, BRANCH_W)


def _attend_sample(q3, k3, v3, cache_k, cache_v, layer, page_table, sc):
    db, t_new, _ = q3.shape
    q4 = q3.reshape(db, t_new, N_HEADS, HEAD) * (HEAD ** -0.5)
    qblk = jnp.einsum("bthd,hg->bhdgt", q4, jnp.eye(N_HEADS, dtype=F32)).reshape(db, BRANCH_W, N_HEADS * t_new)
    qblk = jnp.pad(qblk, ((0, 0), (0, 0), (0, QL - N_HEADS * t_new)))
    qdiag = jnp.transpose(qblk[:, :, :N_HEADS * t_new], (0, 2, 1))
    kmean, o_part, m, l = _moba_pages(page_table, cache_k, cache_v, layer, qdiag, sc["colb"], sc["rowc"],
                                      sc["slc"], sc["head_mask"], t_new)
    nblk = o_part.shape[1]
    rows = lambda a: jnp.transpose(a[:, :, :nblk], (0, 2, 1))
    lanes = lambda a, fill: jnp.pad(rows(a), ((0, 0), (0, 0), (0, QL - N_HEADS * t_new)), constant_values=fill)
    return _moba_combine(rows(kmean), lanes(m, 0.0), lanes(l, 1.0), jnp.transpose(o_part, (0, 2, 1, 3)),
                         qblk, k3, v3, sc["new_bias"], sc["expand"])


def _sample_consts(t_new, past, slopes):
    lane = jnp.arange(QL)
    used = lane < N_HEADS * t_new
    head = jnp.where(used, lane // t_new, 0)
    step = (lane % t_new).astype(F32)
    slope_lane = jnp.where(used, slopes[head], 0.0).astype(F32)
    new_key = jnp.arange(t_new, dtype=F32)
    new_bias = jnp.where(new_key[:, None] <= step[None, :],
                         slope_lane[None, :] * (new_key[:, None] - step[None, :]), NEG)
    out_head = jnp.arange(BRANCH_W) // HEAD
    expand = jnp.stack([(used & (lane % t_new == t))[:, None] & (head[:, None] == out_head[None, :])
                        for t in range(t_new)]).astype(BF16)
    row_head = jnp.arange(N_HEADS * t_new) // t_new
    row_step = (jnp.arange(N_HEADS * t_new) % t_new).astype(F32)
    row_slope = slopes[row_head].astype(F32)
    colb = row_slope[:, None] * jnp.arange(MOBA_BLOCK, dtype=F32)[None, :]
    wide = lambda a: jnp.broadcast_to(a[:, None], (a.shape[0], QL)).astype(F32)
    return dict(new_bias=new_bias.astype(F32), expand=expand, colb=colb,
                rowc=wide(-row_slope * (past + row_step)), slc=wide(row_slope),
                head_mask=(jnp.arange(N_HEADS)[:, None] == out_head[None, :]).astype(F32))


def kernel(x_prompt, x_sample, cache_k, cache_v, page_table, state_conv, state_shift, state_wkv, norm1_g, w_in, conv_w, conv_b, conv_ln_g, conv_ln_b, rwkv_mu, rwkv_w0, rwkv_w_up, rwkv_a0, rwkv_a_up, rwkv_g_up, rwkv_k_k, rwkv_k_a, rwkv_r_k, rwkv_gn_g, rwkv_gn_b, q_norm_g, k_norm_g, w_branch, w_out, norm2_g, moe_w_group, moe_b_group, moe_w_expert, moe_b_expert, moe_w_gate, moe_w_up, moe_w_down):
    stacked = dict(norm1_g=norm1_g, w_in=w_in, conv_w=conv_w, conv_b=conv_b, conv_ln_g=conv_ln_g,
                   conv_ln_b=conv_ln_b, rwkv_mu=rwkv_mu, rwkv_w0=rwkv_w0, rwkv_w_up=rwkv_w_up, rwkv_a0=rwkv_a0,
                   rwkv_a_up=rwkv_a_up, rwkv_g_up=rwkv_g_up, rwkv_k_k=rwkv_k_k, rwkv_k_a=rwkv_k_a,
                   rwkv_r_k=rwkv_r_k, rwkv_gn_g=rwkv_gn_g, rwkv_gn_b=rwkv_gn_b, q_norm_g=q_norm_g,
                   k_norm_g=k_norm_g, w_branch=w_branch, w_out=w_out, norm2_g=norm2_g, moe_w_group=moe_w_group,
                   moe_b_group=moe_b_group, moe_w_expert=moe_w_expert, moe_b_expert=moe_b_expert,
                   moe_w_gate=moe_w_gate, moe_w_up=moe_w_up, moe_w_down=moe_w_down)
    depth = w_in.shape[0]
    b, s, d = x_prompt.shape
    db, t_new, _ = x_sample.shape
    past = page_table.shape[1] * PAGE

    slopes = jnp.exp2(-8.0 * (jnp.arange(N_HEADS, dtype=F32) + 1.0) / N_HEADS)
    slopes_q = jnp.broadcast_to(slopes[:, None, None], (N_HEADS, 1, MOBA_BLOCK))
    e256 = _seg_ones()
    sc = _sample_consts(t_new, past, slopes)
    eye = (jnp.arange(HEAD)[:, None] == (jnp.arange(BRANCH_W) % HEAD)[None, :]).astype(F32)
    cache_kt = jnp.transpose(cache_k, (0, 1, 3, 4, 2))
    cache_vt = jnp.transpose(cache_v, (0, 1, 3, 4, 2))

    tiles_p = (512, 512, b, 256, 1024)
    tiles_s = (db * t_new, t_new, 2, t_new, db * t_new)
    xp, xs = x_prompt, x_sample
    outs = [[] for _ in range(10)]
    for l in range(depth):
        c = _layer_consts({k: v[l] for k, v in stacked.items()})
        xp, kp, vp, cp, sp, wp = _trunk_layer(
            xp, c, jnp.zeros((b, CONV_WIDTH - 1, BRANCH_W), F32), jnp.zeros((b, B_COLS), F32),
            jnp.zeros((b, N_HEADS, HEAD, HEAD), F32), functools.partial(_attend_prompt, slopes_q=slopes_q),
            e256, eye, tiles_p)
        attend_s = functools.partial(_attend_sample, cache_k=cache_kt, cache_v=cache_vt, layer=l,
                                     page_table=page_table, sc=sc)
        xs, ks, vs, cs, ss, ws = _trunk_layer(xs, c, state_conv[l], state_shift[l], state_wkv[l], attend_s,
                                              e256, eye, tiles_s)
        new = (kp.reshape(b, s // PAGE, PAGE, N_HEADS, HEAD), vp.reshape(b, s // PAGE, PAGE, N_HEADS, HEAD),
               ks, vs, cp, cs, sp, ss, wp, ws)
        for acc, val in zip(outs, new):
            acc.append(val)
    return (xp, xs) + tuple(jnp.stack(a) for a in outs)
```

```python
import functools
import math

import jax
import jax.numpy as jnp
from jax import lax
from jax.experimental import pallas as pl
from jax.experimental.pallas import tpu as pltpu

F32 = jnp.float32
BF16 = jnp.bfloat16
HIGHEST = lax.Precision.HIGHEST

HEAD = 64
N_HEADS = 8
BRANCH_W = N_HEADS * HEAD
CONV_WIDTH = 31
LORA = 64
LORA_G = 128
B_COLS = 3 * BRANCH_W + 2 * LORA + LORA_G
MOBA_BLOCK = 256
MOBA_TOPK = 3
Q_BLOCK = 128
PAGE = 128
N_GROUPS = 4
PER_GROUP = 4
N_EXPERTS = N_GROUPS * PER_GROUP
EPS = 1e-6
GN_EPS = 64e-5
NEG = -1e30
MXU_HALF = 256
VMEM_LIMIT = 56 * 1024 * 1024


def _sigmoid(x):
    return 1.0 / (1.0 + jnp.exp(-x))


def _split_dot(x, e, passes):
    acc = None
    rem = x
    for i in range(passes):
        hi = rem.astype(BF16)
        d = jnp.dot(hi, e, preferred_element_type=F32)
        acc = d if acc is None else acc + d
        if i + 1 < passes:
            rem = rem - hi.astype(F32)
    return acc


def _seg_sum(x, e256, passes):
    parts = [_split_dot(x[:, i:i + MXU_HALF], e256, passes) for i in range(0, x.shape[1], MXU_HALF)]
    return jnp.concatenate(parts, axis=1)


def _params(*sem):
    return pltpu.CompilerParams(dimension_semantics=sem, vmem_limit_bytes=VMEM_LIMIT)


def _full(shape):
    n = len(shape)
    return pl.BlockSpec(shape, lambda *_: (0,) * n)


def _inproj_kernel(x_ref, g_ref, w_ref, qg_ref, kg_ref, e_ref, hglu_ref, p_ref, q_ref, k_ref, v_ref):
    x = x_ref[...]
    h = (x * lax.rsqrt(jnp.mean(x * x, axis=-1, keepdims=True) + EPS) * g_ref[...]).astype(BF16)
    c0, c1, c2 = 2 * BRANCH_W, 2 * BRANCH_W + B_COLS, 2 * BRANCH_W + B_COLS + 3 * BRANCH_W
    u = jnp.dot(h, w_ref[:, :c0], preferred_element_type=F32)
    hglu_ref[...] = u[:, :BRANCH_W] * _sigmoid(u[:, BRANCH_W:])
    p_ref[...] = jnp.dot(h, w_ref[:, c0:c1], preferred_element_type=F32)
    qkv = jnp.dot(h, w_ref[:, c1:c2], preferred_element_type=F32)
    e = e_ref[...]
    q = qkv[:, :BRANCH_W]
    k = qkv[:, BRANCH_W:2 * BRANCH_W]
    q_ref[...] = q * lax.rsqrt(_seg_sum(q * q, e, 2) * (1.0 / HEAD) + EPS) * qg_ref[...]
    k_ref[...] = k * lax.rsqrt(_seg_sum(k * k, e, 2) * (1.0 / HEAD) + EPS) * kg_ref[...]
    v_ref[...] = qkv[:, 2 * BRANCH_W:]


def _inproj(x2, g1, w_abc, qg, kg, e256, tm):
    n, d = x2.shape
    row = lambda w: pl.BlockSpec((tm, w), lambda i: (i, 0))
    outs = [BRANCH_W, B_COLS, BRANCH_W, BRANCH_W, BRANCH_W]
    return pl.pallas_call(
        _inproj_kernel,
        grid=(n // tm,),
        in_specs=[row(d), _full(g1.shape), _full(w_abc.shape), _full(qg.shape), _full(kg.shape), _full(e256.shape)],
        out_specs=[row(w) for w in outs],
        out_shape=[jax.ShapeDtypeStruct((n, w), F32) for w in outs],
        compiler_params=_params("parallel"),
        name="inproj",
    )(x2, g1, w_abc, qg, kg, e256)


HALO = 32


def _conv_kernel(h_ref, halo_ref, buf_ref, cw_ref, cb_ref, lg_ref, lb_ref, o_ref, ext):
    i = pl.program_id(1)
    tt = h_ref.shape[1]

    @pl.when(i == 0)
    def _():
        ext[0:HALO, :] = buf_ref[0]

    @pl.when(i > 0)
    def _():
        ext[0:HALO, :] = halo_ref[0]

    ext[HALO:, :] = h_ref[0]
    first = HALO - (CONV_WIDTH - 1)
    acc = ext[first:first + tt, :] * cw_ref[0:1, :]
    for w in range(1, CONV_WIDTH):
        acc = acc + ext[first + w:first + w + tt, :] * cw_ref[w:w + 1, :]
    y = acc + cb_ref[...]
    mu = jnp.mean(y, axis=-1, keepdims=True)
    d = y - mu
    var = jnp.mean(d * d, axis=-1, keepdims=True)
    y = d * lax.rsqrt(var + 1e-5) * lg_ref[...] + lb_ref[...]
    o_ref[0] = y * _sigmoid(y)


def _conv(hglu, halo_src, buf32, cw, cb, lg, lb, tt):
    b, t, c = hglu.shape
    per = max(tt // HALO, 1)
    return pl.pallas_call(
        _conv_kernel,
        grid=(b, t // tt),
        in_specs=[
            pl.BlockSpec((1, tt, c), lambda bi, i: (bi, i, 0)),
            pl.BlockSpec((1, HALO, c), lambda bi, i: (bi, jnp.maximum(i * per - 1, 0), 0)),
            pl.BlockSpec((1, HALO, c), lambda bi, i: (bi, 0, 0)),
            _full(cw.shape), _full(cb.shape), _full(lg.shape), _full(lb.shape),
        ],
        out_specs=pl.BlockSpec((1, tt, c), lambda bi, i: (bi, i, 0)),
        out_shape=jax.ShapeDtypeStruct((b, t, c), F32),
        scratch_shapes=[pltpu.VMEM((HALO + tt, c), F32)],
        compiler_params=_params("parallel", "arbitrary"),
        name="conv",
    )(hglu, halo_src, buf32, cw, cb, lg, lb)


STEP_GROUP = 8


def _wkv_kernel(p_ref, shift_ref, s0_ref, mu_ref, w0_ref, a0_ref, wup_ref, aup_ref, gup_ref,
                kkw_ref, ka_ref, rk_ref, gng_ref, gnb_ref, e_ref, eye_ref,
                o_ref, sout_ref,
                state, pext, r_s, w_s, k_s, kk_s, b_s, v_s, g_s, o_s):
    c = pl.program_id(1)
    nb, tc = p_ref.shape[0], p_ref.shape[1]
    e = e_ref[...]
    w_off = 3 * BRANCH_W

    @pl.when(c == 0)
    def _():
        state[...] = s0_ref[...]

    for b in range(nb):
        @pl.when(c == 0)
        def _():
            pext[b, 7:8, :] = shift_ref[b]

        @pl.when(c > 0)
        def _():
            pext[b, 7:8, :] = pext[b, tc + 7:tc + 8, :]

        p = p_ref[b]
        pext[b, 8:tc + 8, :] = p
        xm = p + (pext[b, 7:tc + 7, :] - p) * mu_ref[...]
        r = xm[:, :BRANCH_W]
        k = xm[:, BRANCH_W:2 * BRANCH_W]
        v = xm[:, 2 * BRANCH_W:w_off]
        wa = xm[:, w_off:w_off + 2 * LORA]
        gd = xm[:, w_off + 2 * LORA:]
        w_raw = w0_ref[...] + jnp.dot(jnp.tanh(wa), wup_ref[...], precision=HIGHEST, preferred_element_type=F32)
        a = _sigmoid(a0_ref[...] + jnp.dot(wa, aup_ref[...], precision=HIGHEST, preferred_element_type=F32))
        g = jnp.dot(_sigmoid(gd), gup_ref[...], precision=HIGHEST, preferred_element_type=F32)
        kk = k * kkw_ref[...]
        kk = kk / jnp.maximum(jnp.sqrt(_seg_sum(kk * kk, e, 3)), 1e-12)
        r_s[b] = r
        w_s[b] = jnp.exp(-math.exp(-0.5) * _sigmoid(w_raw))
        k_s[b] = k * (1.0 + (a - 1.0) * ka_ref[...])
        kk_s[b] = kk
        b_s[b] = kk * a
        v_s[b] = v
        g_s[b] = g

    eye = eye_ref[...]
    e2 = jnp.concatenate([e, e], axis=0)

    def bc(block, j):
        return jnp.broadcast_to(block[j:j + 1, :], (HEAD, BRANCH_W))

    def halves(hi, lo):
        return [jnp.concatenate([hi[:, i:i + MXU_HALF], lo[:, i:i + MXU_HALF]], axis=1)
                for i in range(0, BRANCH_W, MXU_HALF)]

    def unhalve(res, j, n):
        return jnp.concatenate([res[j * HEAD:(j + 1) * HEAD], res[(n + j) * HEAD:(n + j + 1) * HEAD]], axis=1)

    def group(t8, carry):
        base = pl.multiple_of(t8 * STEP_GROUP, STEP_GROUP)
        bs = range(nb)
        kk8, v8, w8, b8, k8, r8 = ([ref[b, pl.ds(base, STEP_GROUP), :] for b in bs]
                                   for ref in (kk_s, v_s, w_s, b_s, k_s, r_s))
        vk = []
        for b in bs:
            v_hi = v8[b].astype(BF16).astype(F32)
            v_lo = v8[b] - v_hi
            hi = jnp.concatenate([(bc(v_hi, j) * eye).astype(BF16) for j in range(STEP_GROUP)], axis=0)
            lo = jnp.concatenate([(bc(v_lo, j) * eye).astype(BF16) for j in range(STEP_GROUP)], axis=0)
            res = jnp.dot(jnp.concatenate(halves(hi, lo), axis=0), e2, preferred_element_type=F32)
            vk.append([unhalve(res, j, STEP_GROUP) * bc(k8[b], j) for j in range(STEP_GROUP)])
        s = [state[b * HEAD:(b + 1) * HEAD, :] for b in bs]
        outs = [[] for _ in bs]
        stack = lambda a: [a[:, :MXU_HALF], a[:, MXU_HALF:]]
        out_row = lambda ro: jnp.sum(ro * eye, axis=0, keepdims=True)
        ys = [None for _ in bs]
        for j in range(STEP_GROUP):
            lhs = []
            for b in bs:
                x = s[b] * bc(kk8[b], j)
                x_hi = x.astype(BF16)
                parts = stack(x_hi) + stack((x - x_hi.astype(F32)).astype(BF16))
                lhs.append(jnp.concatenate(parts + (stack(ys[b]) if j else []), axis=0))
            res = [jnp.dot(lhs[b], e, preferred_element_type=F32) for b in bs]
            for b in bs:
                sk = unhalve(res[b], 0, 1) + unhalve(res[b], 2, 1)
                if j:
                    outs[b].append(out_row(unhalve(res[b], 4, 1)))
                s[b] = s[b] * bc(w8[b], j) - sk * bc(b8[b], j) + vk[b][j]
                ys[b] = (s[b] * bc(r8[b], j)).astype(BF16)
        last = jnp.dot(jnp.concatenate([h for b in bs for h in stack(ys[b])], axis=0), e, preferred_element_type=F32)
        for b in bs:
            outs[b].append(out_row(unhalve(last, 2 * b, 1)))
            state[b * HEAD:(b + 1) * HEAD, :] = s[b]
            o_s[b, pl.ds(base, STEP_GROUP), :] = jnp.concatenate(outs[b], axis=0)
        return carry

    lax.fori_loop(0, tc // STEP_GROUP, group, 0)

    for b in range(nb):
        o = o_s[b]
        d = o - _seg_sum(o, e, 3) * (1.0 / HEAD)
        var = _seg_sum(d * d, e, 3) * (1.0 / HEAD)
        y = d * lax.rsqrt(var + GN_EPS) * gng_ref[...] + gnb_ref[...]
        bonus = _seg_sum(r_s[b] * k_s[b] * rk_ref[...], e, 3) * v_s[b]
        o_ref[b] = (y + bonus) * g_s[b]

    @pl.when(c == pl.num_programs(1) - 1)
    def _():
        sout_ref[...] = state[...]


def _wkv(p, shift, s0, consts, e256, eye, nb, tc):
    b, t, _ = p.shape
    chunk = lambda w: pltpu.VMEM((nb, tc, w), F32)
    return pl.pallas_call(
        _wkv_kernel,
        grid=(b // nb, t // tc),
        in_specs=[
            pl.BlockSpec((nb, tc, B_COLS), lambda bi, c: (bi, c, 0)),
            pl.BlockSpec((nb, 1, B_COLS), lambda bi, c: (bi, 0, 0)),
            pl.BlockSpec((nb * HEAD, BRANCH_W), lambda bi, c: (bi, 0)),
        ] + [_full(a.shape) for a in consts] + [_full(e256.shape), _full(eye.shape)],
        out_specs=[
            pl.BlockSpec((nb, tc, BRANCH_W), lambda bi, c: (bi, c, 0)),
            pl.BlockSpec((nb * HEAD, BRANCH_W), lambda bi, c: (bi, 0)),
        ],
        out_shape=[jax.ShapeDtypeStruct((b, t, BRANCH_W), F32), jax.ShapeDtypeStruct((b * HEAD, BRANCH_W), F32)],
        scratch_shapes=[pltpu.VMEM((nb * HEAD, BRANCH_W), F32), pltpu.VMEM((nb, tc + 8, B_COLS), F32)]
        + [chunk(BRANCH_W) for _ in range(8)],
        compiler_params=_params("parallel", "arbitrary"),
        name="wkv",
    )(p, shift, s0, *consts, e256, eye)


def _top3(gate, valid, idx):
    sel = jnp.zeros(gate.shape, jnp.bool_)
    avail = valid
    big = float(gate.shape[0])
    for _ in range(MOBA_TOPK):
        g = jnp.where(avail, gate, -jnp.inf)
        top = jnp.max(g, axis=0, keepdims=True)
        first = jnp.min(jnp.where(avail & (g == top), idx, big), axis=0, keepdims=True)
        pick = idx == first
        sel = sel | pick
        avail = avail & jnp.logical_not(pick)
    return sel


def _kmean_kernel(k_ref, o_ref):
    k = k_ref[0, 0]
    nb = k.shape[0] // MOBA_BLOCK
    o_ref[0, 0] = jnp.mean(k.reshape(nb, MOBA_BLOCK, HEAD), axis=1)


def _kmean(k_heads):
    b, h, s, d = k_heads.shape
    nb = s // MOBA_BLOCK
    return pl.pallas_call(
        _kmean_kernel,
        grid=(b, h),
        in_specs=[pl.BlockSpec((1, 1, s, d), lambda bi, hi: (bi, hi, 0, 0))],
        out_specs=pl.BlockSpec((1, 1, nb, d), lambda bi, hi: (bi, hi, 0, 0)),
        out_shape=jax.ShapeDtypeStruct((b, h, nb, d), F32),
        compiler_params=_params("parallel", "parallel"),
        name="kmean",
    )(k_heads)


HEAD_GROUP = 4
LOG2E = 1.4426950408889634


def _moba_prompt_kernel(q_ref, k_ref, vt_ref, km_ref, slope_ref, o_ref, selb):
    j_own = pl.program_id(2)
    hg = q_ref.shape[1]
    nt = (((1,), (1,)), ((), ()))
    key = lax.broadcasted_iota(jnp.int32, (MOBA_BLOCK, MOBA_BLOCK), 0)
    qry = lax.broadcasted_iota(jnp.int32, (MOBA_BLOCK, MOBA_BLOCK), 1)
    rel = (key - qry).astype(F32)
    causal = key <= qry

    hs = range(hg)
    slope2 = [slope_ref[h] * LOG2E for h in hs]
    qf = [q_ref[0, h] * (HEAD ** -0.5) for h in hs]
    qs = [(qf[h] * LOG2E).astype(BF16) for h in hs]
    col_bias = [slope2[h] * key[:, 0:1].astype(F32) for h in hs]

    def qk(h, start):
        return lax.dot_general(k_ref[0, h, pl.ds(start, MOBA_BLOCK), :], qs[h], nt, preferred_element_type=F32)

    def pv(h, start, p):
        return jnp.dot(vt_ref[0, h, :, pl.ds(start, MOBA_BLOCK)], p.astype(BF16), preferred_element_type=F32)

    own = pl.multiple_of(j_own * MOBA_BLOCK, MOBA_BLOCK)
    s_own = [qk(h, own) for h in hs]
    for h in hs:
        gate = lax.dot_general(km_ref[0, h], qf[h], nt, precision=HIGHEST, preferred_element_type=F32)
        blk = lax.broadcasted_iota(jnp.int32, gate.shape, 0)
        sel = _top3(gate, blk < j_own, blk.astype(F32))
        dist = ((blk - j_own) * MOBA_BLOCK - qry[0:1, :]).astype(F32)
        selb[h] = jnp.where(sel, slope2[h] * dist, NEG)
    ms, ls, ps = [], [], []
    for h in hs:
        s = jnp.where(causal, s_own[h] + slope2[h] * rel, NEG)
        m = jnp.max(s, axis=0, keepdims=True)
        p = jnp.exp2(s - m)
        ms.append(m)
        ls.append(jnp.sum(p, axis=0, keepdims=True))
        ps.append(p)
    accs = [pv(h, own, ps[h]) for h in hs]

    def body(n, carry):
        ms, ls, accs = carry
        start = pl.multiple_of(n * MOBA_BLOCK, MOBA_BLOCK)
        ss = [qk(h, start) for h in hs]
        m_out, l_out, alphas, ps = [], [], [], []
        for h in hs:
            s = ss[h] + col_bias[h] + selb[h, pl.ds(n, 1), :]
            m_new = jnp.maximum(ms[h], jnp.max(s, axis=0, keepdims=True))
            alpha = jnp.exp2(ms[h] - m_new)
            p = jnp.exp2(s - m_new)
            m_out.append(m_new)
            l_out.append(alpha * ls[h] + jnp.sum(p, axis=0, keepdims=True))
            alphas.append(alpha)
            ps.append(p)
        acc_out = [alphas[h] * accs[h] + pv(h, start, ps[h]) for h in hs]
        return tuple(m_out), tuple(l_out), tuple(acc_out)

    ms, ls, accs = lax.fori_loop(0, j_own, body, (tuple(ms), tuple(ls), tuple(accs)))
    for h in hs:
        o_ref[0, h] = accs[h] / ls[h]


def _moba_prompt(q_heads, k_heads_bf, vt_heads_bf, kmean, slopes):
    b, h, s, d = q_heads.shape
    nb = s // MOBA_BLOCK
    hg = HEAD_GROUP
    return pl.pallas_call(
        _moba_prompt_kernel,
        grid=(b, h // hg, nb),
        in_specs=[
            pl.BlockSpec((1, hg, MOBA_BLOCK, d), lambda bi, hi, i: (bi, hi, i, 0)),
            pl.BlockSpec((1, hg, s, d), lambda bi, hi, i: (bi, hi, 0, 0)),
            pl.BlockSpec((1, hg, d, s), lambda bi, hi, i: (bi, hi, 0, 0)),
            pl.BlockSpec((1, hg, nb, d), lambda bi, hi, i: (bi, hi, 0, 0)),
            pl.BlockSpec((hg, 1, MOBA_BLOCK), lambda bi, hi, i: (hi, 0, 0)),
        ],
        out_specs=pl.BlockSpec((1, hg, d, MOBA_BLOCK), lambda bi, hi, i: (bi, hi, 0, i)),
        out_shape=jax.ShapeDtypeStruct((b, h, d, s), F32),
        scratch_shapes=[pltpu.VMEM((hg, nb, MOBA_BLOCK), F32)],
        compiler_params=_params("parallel", "parallel", "arbitrary"),
        name="moba_prompt",
    )(q_heads, k_heads_bf, vt_heads_bf, kmean, slopes)


QL = 128


BLOCKS_PER_STEP = 4
PAGES_PER_BLOCK = MOBA_BLOCK // PAGE


def _moba_pages_kernel(pt_ref, *refs):
    npg = BLOCKS_PER_STEP * PAGES_PER_BLOCK
    k_refs, v_refs = refs[:npg], refs[npg:2 * npg]
    qd_ref, colb_ref, rowc_ref, slc_ref, hm_ref, km_ref, o_ref, m_ref, l_ref = refs[2 * npg:]
    n = pl.program_id(1)
    t_new = o_ref.shape[2]

    @pl.when(n == 0)
    def _():
        km_ref[...] = jnp.zeros(km_ref.shape, F32)
        m_ref[...] = jnp.zeros(m_ref.shape, F32)
        l_ref[...] = jnp.ones(l_ref.shape, F32)

    q = qd_ref[0].astype(BF16)
    lane = lax.broadcasted_iota(jnp.int32, (1, QL), 1)
    nt = (((1,), (1,)), ((), ()))
    us = range(BLOCKS_PER_STEP)
    pages = [range(u * PAGES_PER_BLOCK, (u + 1) * PAGES_PER_BLOCK) for u in us]
    block = lambda refs, u: jnp.concatenate([refs[i][0, 0].reshape(BRANCH_W, PAGE) for i in pages[u]], axis=1)
    kts = [block(k_refs, u) for u in us]
    ss = [jnp.dot(q, kts[u].astype(BF16), preferred_element_type=F32) for u in us]
    ps = []
    for u in us:
        blk = n * BLOCKS_PER_STEP + u
        here = lane == blk
        km_ref[0] = jnp.where(here, jnp.sum(kts[u], axis=1, keepdims=True) * (1.0 / MOBA_BLOCK), km_ref[0])
        s = ss[u] + colb_ref[...] + (rowc_ref[:, 0:1] + slc_ref[:, 0:1] * (blk * MOBA_BLOCK).astype(F32))
        m = jnp.max(s, axis=1, keepdims=True)
        p = jnp.exp(s - m)
        m_ref[0] = jnp.where(here, m, m_ref[0])
        l_ref[0] = jnp.where(here, jnp.sum(p, axis=1, keepdims=True), l_ref[0])
        ps.append(p.astype(BF16))
    for u in us:
        full = lax.dot_general(ps[u], block(v_refs, u).astype(BF16), nt, preferred_element_type=F32)
        acc = full[0:t_new, :] * hm_ref[0:1, :]
        for h in range(1, N_HEADS):
            acc = acc + full[h * t_new:(h + 1) * t_new, :] * hm_ref[h:h + 1, :]
        o_ref[0, u] = acc


def _moba_pages(page_table, cache_kt, cache_vt, layer, qdiag, colb, rowc, slc, head_mask, t_new):
    db, n_pages = page_table.shape
    nblk = n_pages // PAGES_PER_BLOCK
    npg = BLOCKS_PER_STEP * PAGES_PER_BLOCK
    rows = qdiag.shape[1]
    page = lambda off: pl.BlockSpec((1, 1, N_HEADS, HEAD, PAGE),
                                    lambda b, n, pt: (layer, pt[b, npg * n + off], 0, 0, 0))
    const = lambda a: pl.BlockSpec(a.shape, lambda b, n, pt: (0, 0))
    stat = lambda r: pl.BlockSpec((1, r, QL), lambda b, n, pt: (b, 0, 0))
    return pl.pallas_call(
        _moba_pages_kernel,
        grid_spec=pltpu.PrefetchScalarGridSpec(
            num_scalar_prefetch=1,
            grid=(db, nblk // BLOCKS_PER_STEP),
            in_specs=[page(i) for i in range(npg)] * 2
            + [pl.BlockSpec((1, rows, BRANCH_W), lambda b, n, pt: (b, 0, 0)),
               const(colb), const(rowc), const(slc), const(head_mask)],
            out_specs=[stat(BRANCH_W),
                       pl.BlockSpec((1, BLOCKS_PER_STEP, t_new, BRANCH_W), lambda b, n, pt: (b, n, 0, 0)),
                       stat(rows), stat(rows)],
        ),
        out_shape=[jax.ShapeDtypeStruct((db, BRANCH_W, QL), F32),
                   jax.ShapeDtypeStruct((db, nblk, t_new, BRANCH_W), F32),
                   jax.ShapeDtypeStruct((db, rows, QL), F32),
                   jax.ShapeDtypeStruct((db, rows, QL), F32)],
        compiler_params=_params("parallel", "arbitrary"),
        name="moba_pages",
    )(page_table, *([cache_kt] * npg), *([cache_vt] * npg), qdiag, colb, rowc, slc, head_mask)


def _moba_combine_kernel(km_ref, m_ref, l_ref, op_ref, qb_ref, kn_ref, vn_ref, nb_ref, ex_ref, o_ref):
    t_new = kn_ref.shape[1]
    qb = qb_ref[0]
    gate = jnp.dot(km_ref[0], qb, precision=HIGHEST, preferred_element_type=F32)
    blk = lax.broadcasted_iota(jnp.int32, gate.shape, 0)
    sel = _top3(gate, blk >= 0, blk.astype(F32))
    m_blk = jnp.where(sel, m_ref[0], NEG)
    s_new = jnp.dot(kn_ref[0].astype(BF16), qb.astype(BF16), preferred_element_type=F32) + nb_ref[...]
    m_all = jnp.maximum(jnp.max(m_blk, axis=0, keepdims=True), jnp.max(s_new, axis=0, keepdims=True))
    w_blk = jnp.where(sel, jnp.exp(m_blk - m_all), 0.0)
    p_new = jnp.exp(s_new - m_all)
    denom = jnp.sum(w_blk * l_ref[0], axis=0, keepdims=True) + jnp.sum(p_new, axis=0, keepdims=True)
    vn = vn_ref[0]
    outs = []
    for t in range(t_new):
        ex = ex_ref[t]
        wt = _split_dot(w_blk, ex, 3)
        pt = _split_dot(p_new, ex, 3)
        num = (jnp.sum(wt * op_ref[0, t], axis=0, keepdims=True)
               + jnp.sum(pt * vn, axis=0, keepdims=True))
        outs.append(num / _split_dot(denom, ex, 3))
    o_ref[0] = jnp.concatenate(outs, axis=0)


def _moba_combine(kmean, m, l, o_part, qblk, k_new, v_new, new_bias, expand):
    db, t_new, nblk, _ = o_part.shape
    lead = lambda shape: pl.BlockSpec((1,) + shape, lambda b: (b,) + (0,) * len(shape))
    return pl.pallas_call(
        _moba_combine_kernel,
        grid=(db,),
        in_specs=[lead((nblk, BRANCH_W)), lead((nblk, QL)), lead((nblk, QL)), lead((t_new, nblk, BRANCH_W)),
                  lead((BRANCH_W, QL)), lead((t_new, BRANCH_W)), lead((t_new, BRANCH_W)),
                  _full(new_bias.shape), _full(expand.shape)],
        out_specs=lead((t_new, BRANCH_W)),
        out_shape=jax.ShapeDtypeStruct((db, t_new, BRANCH_W), F32),
        compiler_params=_params("parallel"),
        name="moba_combine",
    )(kmean, m, l, o_part, qblk, k_new, v_new, new_bias, expand)


ROUTE_LANES = 128


def _merge_kernel(x_ref, oa_ref, ob_ref, oc_ref, g1_ref, wg_ref, wb_ref, wo_ref, g2_ref, wr_ref, br_ref,
                  x1_ref, h2_ref, comb_ref):
    x = x_ref[...]
    d = x.shape[1]
    h = (x * lax.rsqrt(jnp.mean(x * x, axis=-1, keepdims=True) + EPS) * g1_ref[...]).astype(BF16)
    merged = None
    for n, br_ref_n in enumerate((oa_ref, ob_ref, oc_ref)):
        gate = _sigmoid(jnp.dot(h, wg_ref[:, n * d:(n + 1) * d], preferred_element_type=F32))
        br = jnp.dot(br_ref_n[...].astype(BF16), wb_ref[n], preferred_element_type=F32)
        merged = gate * br if merged is None else merged + gate * br
    x1 = x + jnp.dot(merged.astype(BF16), wo_ref[...], preferred_element_type=F32)
    x1_ref[...] = x1
    h2 = x1 * lax.rsqrt(jnp.mean(x1 * x1, axis=-1, keepdims=True) + EPS) * g2_ref[...]
    h2_ref[...] = h2.astype(BF16)

    logits = jnp.dot(h2, wr_ref[...], precision=HIGHEST, preferred_element_type=F32) + br_ref[...]
    lane = lax.broadcasted_iota(jnp.int32, logits.shape, 1)
    lane_f = lane.astype(F32)
    is_g = (lane >= N_EXPERTS) & (lane < N_EXPERTS + N_GROUPS)
    glog = jnp.where(is_g, logits, -jnp.inf)
    gmax = jnp.max(glog, axis=-1, keepdims=True)
    gidx = jnp.min(jnp.where(glog == gmax, lane_f - N_EXPERTS, float(N_GROUPS)), axis=-1, keepdims=True)
    gw = 1.0 / jnp.sum(jnp.where(is_g, jnp.exp(logits - gmax), 0.0), axis=-1, keepdims=True)
    in_grp = (lane < N_EXPERTS) & ((lane // PER_GROUP).astype(F32) == gidx)
    big = float(ROUTE_LANES)
    m1 = jnp.where(in_grp, logits, -jnp.inf)
    t1 = jnp.max(m1, axis=-1, keepdims=True)
    i1 = jnp.min(jnp.where(in_grp & (m1 == t1), lane_f, big), axis=-1, keepdims=True)
    rest = in_grp & (lane_f != i1)
    m2 = jnp.where(rest, logits, -jnp.inf)
    t2 = jnp.max(m2, axis=-1, keepdims=True)
    i2 = jnp.min(jnp.where(rest & (m2 == t2), lane_f, big), axis=-1, keepdims=True)
    e2 = jnp.exp(t2 - t1)
    w1 = 1.0 / (1.0 + e2)
    comb_ref[...] = (jnp.where(lane_f == i1, w1, 0.0) + jnp.where(lane_f == i2, e2 * w1, 0.0)) * gw


def _merge(x2, oa, ob, oc, g1, w_gate, w_branch, w_out, g2, w_route, b_route, tm):
    n, d = x2.shape
    row = lambda w: pl.BlockSpec((tm, w), lambda i: (i, 0))
    return pl.pallas_call(
        _merge_kernel,
        grid=(n // tm,),
        in_specs=[row(d), row(BRANCH_W), row(BRANCH_W), row(BRANCH_W), _full(g1.shape), _full(w_gate.shape),
                  _full(w_branch.shape), _full(w_out.shape), _full(g2.shape), _full(w_route.shape),
                  _full(b_route.shape)],
        out_specs=[row(d), row(d), row(ROUTE_LANES)],
        out_shape=[jax.ShapeDtypeStruct((n, d), F32), jax.ShapeDtypeStruct((n, d), BF16),
                   jax.ShapeDtypeStruct((n, ROUTE_LANES), F32)],
        compiler_params=_params("parallel"),
        name="merge",
    )(x2, oa, ob, oc, g1, w_gate, w_branch, w_out, g2, w_route, b_route)


def _moe_kernel(h_ref, comb_ref, x1_ref, wg_ref, wu_ref, wd_ref, o_ref):
    e = pl.program_id(1)

    @pl.when(e == 0)
    def _():
        o_ref[...] = x1_ref[...]

    comb = comb_ref[...]
    lane = lax.broadcasted_iota(jnp.int32, comb.shape, 1)
    c = jnp.sum(jnp.where(lane == e, comb, 0.0), axis=-1, keepdims=True)
    h = h_ref[...]
    hg = jnp.dot(h, wg_ref[0], preferred_element_type=F32)
    hu = jnp.dot(h, wu_ref[0], preferred_element_type=F32)
    act = hg * _sigmoid(hg) * hu * c
    o_ref[...] += jnp.dot(act.astype(BF16), wd_ref[0], preferred_element_type=F32)


def _moe(h2, comb, x1, w_gate, w_up, w_down, tm):
    n, d = x1.shape
    ne, _, f = w_gate.shape
    row = lambda w: pl.BlockSpec((tm, w), lambda i, e: (i, 0))
    return pl.pallas_call(
        _moe_kernel,
        grid=(n // tm, ne),
        in_specs=[row(d), row(ROUTE_LANES), row(d),
                  pl.BlockSpec((1, d, f), lambda i, e: (e, 0, 0)),
                  pl.BlockSpec((1, d, f), lambda i, e: (e, 0, 0)),
                  pl.BlockSpec((1, f, d), lambda i, e: (e, 0, 0))],
        out_specs=row(d),
        out_shape=jax.ShapeDtypeStruct((n, d), F32),
        compiler_params=_params("parallel", "arbitrary"),
        name="moe",
    )(h2, comb, x1, w_gate, w_up, w_down)


def _seg_ones():
    i = jnp.arange(MXU_HALF) // HEAD
    return (i[:, None] == i[None, :]).astype(BF16)


def _layer_consts(prm):
    row = lambda a: a.reshape(1, -1).astype(F32)
    w_in = prm["w_in"]
    off_g = 2 * BRANCH_W + B_COLS + 3 * BRANCH_W
    zeros = jnp.zeros((LORA, BRANCH_W), F32)
    c = dict(
        g1=row(prm["norm1_g"]), g2=row(prm["norm2_g"]),
        w_abc=w_in[:, :off_g].astype(BF16), w_gate=w_in[:, off_g:].astype(BF16),
        qg=row(jnp.tile(prm["q_norm_g"], N_HEADS)), kg=row(jnp.tile(prm["k_norm_g"], N_HEADS)),
        cw=jnp.pad(prm["conv_w"], ((0, 1), (0, 0))), cb=row(prm["conv_b"]),
        lg=row(prm["conv_ln_g"]), lb=row(prm["conv_ln_b"]),
        wkv=[row(prm["rwkv_mu"]), row(prm["rwkv_w0"]), row(prm["rwkv_a0"]),
             jnp.concatenate([prm["rwkv_w_up"], zeros], axis=0), jnp.concatenate([zeros, prm["rwkv_a_up"]], axis=0),
             prm["rwkv_g_up"], row(prm["rwkv_k_k"]), row(prm["rwkv_k_a"]), row(prm["rwkv_r_k"]),
             row(prm["rwkv_gn_g"]), row(prm["rwkv_gn_b"])],
        w_branch=prm["w_branch"].astype(BF16), w_out=prm["w_out"].astype(BF16),
        w_route=jnp.pad(jnp.concatenate([prm["moe_w_expert"], prm["moe_w_group"]], axis=1),
                        ((0, 0), (0, ROUTE_LANES - N_EXPERTS - N_GROUPS))),
        b_route=jnp.pad(jnp.concatenate([prm["moe_b_expert"], prm["moe_b_group"]]),
                        (0, ROUTE_LANES - N_EXPERTS - N_GROUPS)).reshape(1, ROUTE_LANES),
        moe_g=prm["moe_w_gate"].astype(BF16), moe_u=prm["moe_w_up"].astype(BF16),
        moe_d=prm["moe_w_down"].astype(BF16),
    )
    return c


def _state_in(wkv):
    b = wkv.shape[0]
    return jnp.transpose(wkv, (0, 2, 1, 3)).reshape(b * HEAD, BRANCH_W)


def _state_out(s, b):
    return jnp.transpose(s.reshape(b, HEAD, N_HEADS, HEAD), (0, 2, 1, 3))


def _trunk_layer(x, c, conv_buf, shift_prev, wkv_prev, attend, e256, eye, tiles):
    b, t, d = x.shape
    tm, tt, nb, tc, tm_moe = tiles
    x2 = x.reshape(b * t, d)
    hglu, p, q, k, v = _inproj(x2, c["g1"], c["w_abc"], c["qg"], c["kg"], e256, tm)
    hglu = hglu.reshape(b, t, BRANCH_W)
    p = p.reshape(b, t, B_COLS)

    buf32 = jnp.pad(conv_buf, ((0, 0), (HALO - (CONV_WIDTH - 1), 0), (0, 0)))
    o_a = _conv(hglu, hglu if t >= HALO else buf32, buf32, c["cw"], c["cb"], c["lg"], c["lb"], tt)
    conv_new = jnp.concatenate([conv_buf, hglu], axis=1)[:, -(CONV_WIDTH - 1):]

    o_b, s_new = _wkv(p, shift_prev[:, None, :], _state_in(wkv_prev), c["wkv"], e256, eye, nb, tc)
    shift_new = p[:, -1]
    wkv_new = _state_out(s_new, b)

    q3 = q.reshape(b, t, BRANCH_W)
    k3 = k.reshape(b, t, BRANCH_W)
    v3 = v.reshape(b, t, BRANCH_W)
    o_c = attend(q3, k3, v3)

    x1, h2, comb = _merge(x2, o_a.reshape(b * t, BRANCH_W), o_b.reshape(b * t, BRANCH_W),
                          o_c.reshape(b * t, BRANCH_W), c["g1"], c["w_gate"], c["w_branch"], c["w_out"],
                          c["g2"], c["w_route"], c["b_route"], tm)
    x_out = _moe(h2, comb, x1, c["moe_g"], c["moe_u"], c["moe_d"], tm_moe).reshape(b, t, d)
    return (x_out, k3.reshape(b, t, N_HEADS, HEAD), v3.reshape(b, t, N_HEADS, HEAD), conv_new, shift_new, wkv_new)


def _attend_prompt(q3, k3, v3, slopes_q):
    b, s, _ = q3.shape
    heads = lambda a: jnp.transpose(a.reshape(b, s, N_HEADS, HEAD), (0, 2, 1, 3))
    qh, kh = heads(q3), heads(k3)
    vt = jnp.transpose(v3.reshape(b, s, N_HEADS, HEAD), (0, 2, 3, 1)).astype(BF16)
    ot = _moba_prompt(qh, kh.astype(BF16), vt, _kmean(kh), slopes_q)
    return jnp.transpose(ot, (0, 3, 1, 2)).reshape(b, s, BRANCH_W)


def _attend_sample(q3, k3, v3, cache_k, cache_v, layer, page_table, sc):
    db, t_new, _ = q3.shape
    q4 = q3.reshape(db, t_new, N_HEADS, HEAD) * (HEAD ** -0.5)
    qblk = jnp.einsum("bthd,hg->bhdgt", q4, jnp.eye(N_HEADS, dtype=F32)).reshape(db, BRANCH_W, N_HEADS * t_new)
    qblk = jnp.pad(qblk, ((0, 0), (0, 0), (0, QL - N_HEADS * t_new)))
    qdiag = jnp.transpose(qblk[:, :, :N_HEADS * t_new], (0, 2, 1))
    kmean, o_part, m, l = _moba_pages(page_table, cache_k, cache_v, layer, qdiag, sc["colb"], sc["rowc"],
                                      sc["slc"], sc["head_mask"], t_new)
    nblk = o_part.shape[1]
    rows = lambda a: jnp.transpose(a[:, :, :nblk], (0, 2, 1))
    lanes = lambda a, fill: jnp.pad(rows(a), ((0, 0), (0, 0), (0, QL - N_HEADS * t_new)), constant_values=fill)
    return _moba_combine(rows(kmean), lanes(m, 0.0), lanes(l, 1.0), jnp.transpose(o_part, (0, 2, 1, 3)),
                         qblk, k3, v3, sc["new_bias"], sc["expand"])


def _sample_consts(t_new, past, slopes):
    lane = jnp.arange(QL)
    used = lane < N_HEADS * t_new
    head = jnp.where(used, lane // t_new, 0)
    step = (lane % t_new).astype(F32)
    slope_lane = jnp.where(used, slopes[head], 0.0).astype(F32)
    new_key = jnp.arange(t_new, dtype=F32)
    new_bias = jnp.where(new_key[:, None] <= step[None, :],
                         slope_lane[None, :] * (new_key[:, None] - step[None, :]), NEG)
    out_head = jnp.arange(BRANCH_W) // HEAD
    expand = jnp.stack([(used & (lane % t_new == t))[:, None] & (head[:, None] == out_head[None, :])
                        for t in range(t_new)]).astype(BF16)
    row_head = jnp.arange(N_HEADS * t_new) // t_new
    row_step = (jnp.arange(N_HEADS * t_new) % t_new).astype(F32)
    row_slope = slopes[row_head].astype(F32)
    colb = row_slope[:, None] * jnp.arange(MOBA_BLOCK, dtype=F32)[None, :]
    wide = lambda a: jnp.broadcast_to(a[:, None], (a.shape[0], QL)).astype(F32)
    return dict(new_bias=new_bias.astype(F32), expand=expand, colb=colb,
                rowc=wide(-row_slope * (past + row_step)), slc=wide(row_slope),
                head_mask=(jnp.arange(N_HEADS)[:, None] == out_head[None, :]).astype(F32))


def kernel(x_prompt, x_sample, cache_k, cache_v, page_table, state_conv, state_shift, state_wkv, norm1_g, w_in, conv_w, conv_b, conv_ln_g, conv_ln_b, rwkv_mu, rwkv_w0, rwkv_w_up, rwkv_a0, rwkv_a_up, rwkv_g_up, rwkv_k_k, rwkv_k_a, rwkv_r_k, rwkv_gn_g, rwkv_gn_b, q_norm_g, k_norm_g, w_branch, w_out, norm2_g, moe_w_group, moe_b_group, moe_w_expert, moe_b_expert, moe_w_gate, moe_w_up, moe_w_down):
    stacked = dict(norm1_g=norm1_g, w_in=w_in, conv_w=conv_w, conv_b=conv_b, conv_ln_g=conv_ln_g,
                   conv_ln_b=conv_ln_b, rwkv_mu=rwkv_mu, rwkv_w0=rwkv_w0, rwkv_w_up=rwkv_w_up, rwkv_a0=rwkv_a0,
                   rwkv_a_up=rwkv_a_up, rwkv_g_up=rwkv_g_up, rwkv_k_k=rwkv_k_k, rwkv_k_a=rwkv_k_a,
                   rwkv_r_k=rwkv_r_k, rwkv_gn_g=rwkv_gn_g, rwkv_gn_b=rwkv_gn_b, q_norm_g=q_norm_g,
                   k_norm_g=k_norm_g, w_branch=w_branch, w_out=w_out, norm2_g=norm2_g, moe_w_group=moe_w_group,
                   moe_b_group=moe_b_group, moe_w_expert=moe_w_expert, moe_b_expert=moe_b_expert,
                   moe_w_gate=moe_w_gate, moe_w_up=moe_w_up, moe_w_down=moe_w_down)
    depth = w_in.shape[0]
    b, s, d = x_prompt.shape
    db, t_new, _ = x_sample.shape
    past = page_table.shape[1] * PAGE

    slopes = jnp.exp2(-8.0 * (jnp.arange(N_HEADS, dtype=F32) + 1.0) / N_HEADS)
    slopes_q = jnp.broadcast_to(slopes[:, None, None], (N_HEADS, 1, MOBA_BLOCK))
    e256 = _seg_ones()
    sc = _sample_consts(t_new, past, slopes)
    eye = (jnp.arange(HEAD)[:, None] == (jnp.arange(BRANCH_W) % HEAD)[None, :]).astype(F32)
    cache_kt = jnp.transpose(cache_k, (0, 1, 3, 4, 2))
    cache_vt = jnp.transpose(cache_v, (0, 1, 3, 4, 2))

    tiles_p = (512, 512, b, 256, 1024)
    tiles_s = (db * t_new, t_new, 2, t_new, db * t_new)
    xp, xs = x_prompt, x_sample
    outs = [[] for _ in range(10)]
    for l in range(depth):
        c = _layer_consts({k: v[l] for k, v in stacked.items()})
        xp, kp, vp, cp, sp, wp = _trunk_layer(
            xp, c, jnp.zeros((b, CONV_WIDTH - 1, BRANCH_W), F32), jnp.zeros((b, B_COLS), F32),
            jnp.zeros((b, N_HEADS, HEAD, HEAD), F32), functools.partial(_attend_prompt, slopes_q=slopes_q),
            e256, eye, tiles_p)
        attend_s = functools.partial(_attend_sample, cache_k=cache_kt, cache_v=cache_vt, layer=l,
                                     page_table=page_table, sc=sc)
        xs, ks, vs, cs, ss, ws = _trunk_layer(xs, c, state_conv[l], state_shift[l], state_wkv[l], attend_s,
                                              e256, eye, tiles_s)
        new = (kp.reshape(b, s // PAGE, PAGE, N_HEADS, HEAD), vp.reshape(b, s // PAGE, PAGE, N_HEADS, HEAD),
               ks, vs, cp, cs, sp, ss, wp, ws)
        for acc, val in zip(outs, new):
            acc.append(val)
    return (xp, xs) + tuple(jnp.stack(a) for a in outs)
```
